```python
import jax, jax.numpy as jnp
from jax import lax
import numpy as np

D_MODEL = 2048
BATCH = 8
SEQ = 4096
DEPTH = 1
DEC_BATCH = 16
DEC_SEQ = 32
PAST_LEN = 2048

CHUNK = 64
Q_BLOCK = 128
PLE_DIM = 256
MLA_HEADS = 8
MLA_Q_RANK = 512
MLA_KV_RANK = 512
MLA_NOPE = 128
MLA_ROPE = 64
MLA_V = 128
MLA_SCALE = (MLA_NOPE + MLA_ROPE) ** -0.5
ROPE_THETA = 10000.0
FOX_HEADS = 8
FOX_HEAD_DIM = 128
FOX_WIDTH = FOX_HEADS * FOX_HEAD_DIM
FOX_SCALE = FOX_HEAD_DIM ** -0.5
D_FF = 5632
CONV_W = 3
EPS = 1e-6
NEG_INF = -1e30

IN_SPLITS = (MLA_Q_RANK, MLA_KV_RANK, MLA_ROPE, FOX_WIDTH, FOX_WIDTH, FOX_WIDTH,
             FOX_HEADS, D_MODEL, D_MODEL)
IN_WIDTH = sum(IN_SPLITS)

kernel_name = "streaming_mla_fox_gated_hybrid"


def rmsnorm(x, g):
    xf = x.astype(jnp.float32)
    y = xf * lax.rsqrt(jnp.mean(xf * xf, axis=-1, keepdims=True) + EPS)
    return (y * g.astype(jnp.float32)).astype(x.dtype)


def rope(x, pos):
    half = x.shape[-1] // 2
    inv = ROPE_THETA ** (-jnp.arange(half, dtype=jnp.float32) / half)
    ang = pos.astype(jnp.float32)[:, None] * inv[None, :]
    ang = ang.reshape((ang.shape[0],) + (1,) * (x.ndim - 3) + (half,))
    cos, sin = jnp.cos(ang), jnp.sin(ang)
    xf = x.astype(jnp.float32)
    x1, x2 = xf[..., :half], xf[..., half:]
    return jnp.concatenate([x1 * cos - x2 * sin, x2 * cos + x1 * sin], axis=-1).astype(x.dtype)


def split_cols(z, widths):
    outs, o = [], 0
    for w in widths:
        outs.append(z[..., o:o + w])
        o += w
    return outs


def sweep_query_blocks(fn, q_args, qpos):
    b, t = q_args[0].shape[:2]
    nb = t // Q_BLOCK
    xs = tuple(jnp.swapaxes(a.reshape((b, nb, Q_BLOCK) + a.shape[2:]), 0, 1) for a in q_args)
    out = lax.map(lambda blk: fn(*blk[0], blk[1]), (xs, qpos.reshape(nb, Q_BLOCK)))
    return jnp.swapaxes(out, 0, 1).reshape((b, t) + out.shape[3:])


def layer(x, pe, past, w):
    b, t, _ = x.shape
    p_len = 0 if past is None else past[0].shape[1]
    qpos = p_len + jnp.arange(t)

    a = rmsnorm(x, w["g_mix"])
    cq, ckv, kr, fq, fk, fv, fl, ga, gb = split_cols(a @ w["w_in"], IN_SPLITS)
    q = (rmsnorm(cq, w["g_q"]) @ w["w_uq"]).reshape(b, t, MLA_HEADS, MLA_NOPE + MLA_ROPE)
    q_nope, q_rope = q[..., :MLA_NOPE], rope(q[..., MLA_NOPE:], qpos)
    ckv = rmsnorm(ckv, w["g_kv"])
    kr = rope(kr, qpos)
    fq = fq.reshape(b, t, FOX_HEADS, FOX_HEAD_DIM)
    fk = fk.reshape(b, t, FOX_HEADS, FOX_HEAD_DIM)
    fv = fv.reshape(b, t, FOX_HEADS, FOX_HEAD_DIM)
    logf = jax.nn.log_sigmoid((fl + w["b_f"]).astype(jnp.float32))

    if past is None:
        ckv_all, kr_all, fk_all, fv_all, logf_all = ckv, kr, fk, fv, logf
    else:
        c_ckv, c_kr, c_fk, c_fv, c_logf, _ = past
        ckv_all = jnp.concatenate([c_ckv.astype(ckv.dtype), ckv], axis=1)
        kr_all = jnp.concatenate([c_kr.astype(kr.dtype), kr], axis=1)
        fk_all = jnp.concatenate([c_fk.astype(fk.dtype), fk], axis=1)
        fv_all = jnp.concatenate([c_fv.astype(fv.dtype), fv], axis=1)
        logf_all = jnp.concatenate([c_logf.astype(jnp.float32), logf], axis=1)
    s_len = p_len + t
    kpos = jnp.arange(s_len)

    kv = (ckv_all @ w["w_ukv"]).reshape(b, s_len, MLA_HEADS, MLA_NOPE + MLA_V)
    k_nope, v_mla = kv[..., :MLA_NOPE], kv[..., MLA_NOPE:]
    kchunk = kpos // CHUNK

    def mla_fn(qn, qr, qp):
        s = (jnp.einsum('bqhd,bkhd->bhqk', qn, k_nope, preferred_element_type=jnp.float32)
             + jnp.einsum('bqhr,bkr->bhqk', qr, kr_all, preferred_element_type=jnp.float32)) * MLA_SCALE
        vis = kchunk[None, :] <= (qp // CHUNK)[:, None]
        s = jnp.where(vis, s, NEG_INF)
        pr = jax.nn.softmax(s, axis=-1).astype(v_mla.dtype)
        return jnp.einsum('bhqk,bkhd->bqhd', pr, v_mla)

    cum = jnp.cumsum(logf_all, axis=1)
    cum_q = cum[:, p_len:]
    cum_k = jnp.swapaxes(cum, 1, 2)[:, :, None, :]

    def fox_fn(qq, cq_, qp):
        s = (jnp.einsum('bqhd,bkhd->bhqk', qq, fk_all, preferred_element_type=jnp.float32) * FOX_SCALE
             + jnp.swapaxes(cq_, 1, 2)[..., None] - cum_k)
        vis = kpos[None, :] <= qp[:, None]
        s = jnp.where(vis, s, NEG_INF)
        pr = jax.nn.softmax(s, axis=-1).astype(fv_all.dtype)
        return jnp.einsum('bhqk,bkhd->bqhd', pr, fv_all)

    if past is None:
        o_a = sweep_query_blocks(mla_fn, (q_nope, q_rope), qpos)
        o_b = sweep_query_blocks(fox_fn, (fq, cum_q), qpos)
    else:
        o_a = mla_fn(q_nope, q_rope, qpos)
        o_b = fox_fn(fq, cum_q, qpos)

    merged = (jax.nn.sigmoid(ga) * (o_a.reshape(b, t, MLA_HEADS * MLA_V) @ w["w_oa"])
              + jax.nn.sigmoid(gb) * (o_b.reshape(b, t, FOX_WIDTH) @ w["w_ob"]))
    h = x + merged @ w["w_o"]

    u = rmsnorm(h, w["g_ffn"]) @ w["w_up"]
    if past is None:
        prev = jnp.zeros((b, CONV_W - 1, 2 * D_FF), u.dtype)
    else:
        prev = past[5].astype(u.dtype)
    u_pad = jnp.concatenate([prev, u], axis=1)
    uc = w["b_conv"] + sum(u_pad[:, k:k + t] * w["w_conv"][k] for k in range(CONV_W))
    val, gate = uc[..., :D_FF], uc[..., D_FF:]
    h = h + (jax.nn.gelu(gate) * val) @ w["w_down"]

    h = h + jax.nn.sigmoid(rmsnorm(h, w["g_ple"]) @ w["w_pg"]) * (pe @ w["w_ple"])

    new_state = (ckv, kr, fk, fv, logf, u_pad[:, -(CONV_W - 1):])
    return h, new_state


def setup_inputs(seed: int = 0) -> dict:
    key = jax.random.key(seed)
    ks = list(jax.random.split(key, 40))

    def nrm(k, shape, scale=1.0):
        return scale * jax.random.normal(k, shape, jnp.float32)

    def gain(k, n):
        return 1.0 + 0.1 * jax.random.normal(k, (DEPTH, n), jnp.float32)

    L = DEPTH
    return {
        "x_prompt": nrm(ks[0], (BATCH, SEQ, D_MODEL)),
        "x_sample": nrm(ks[1], (DEC_BATCH, DEC_SEQ, D_MODEL)),
        "cache_mla_ckv": nrm(ks[2], (L, DEC_BATCH, PAST_LEN, MLA_KV_RANK)),
        "cache_mla_krope": nrm(ks[3], (L, DEC_BATCH, PAST_LEN, MLA_ROPE)),
        "cache_fox_k": nrm(ks[4], (L, DEC_BATCH, PAST_LEN, FOX_HEADS, FOX_HEAD_DIM)),
        "cache_fox_v": nrm(ks[5], (L, DEC_BATCH, PAST_LEN, FOX_HEADS, FOX_HEAD_DIM)),
        "cache_fox_logf": jax.nn.log_sigmoid(2.0 + nrm(ks[6], (L, DEC_BATCH, PAST_LEN, FOX_HEADS), 0.5)),
        "state_ffn_conv": nrm(ks[7], (L, DEC_BATCH, CONV_W - 1, 2 * D_FF)),
        "p_prompt": nrm(ks[8], (L, BATCH, SEQ, PLE_DIM)),
        "p_sample": nrm(ks[9], (L, DEC_BATCH, DEC_SEQ, PLE_DIM)),
        "g_mix": gain(ks[10], D_MODEL),
        "w_in": nrm(ks[11], (L, D_MODEL, IN_WIDTH), D_MODEL ** -0.5),
        "b_f": 2.0 + nrm(ks[12], (L, FOX_HEADS), 0.5),
        "g_q": gain(ks[13], MLA_Q_RANK),
        "w_uq": nrm(ks[14], (L, MLA_Q_RANK, MLA_HEADS * (MLA_NOPE + MLA_ROPE)), MLA_Q_RANK ** -0.5),
        "g_kv": gain(ks[15], MLA_KV_RANK),
        "w_ukv": nrm(ks[16], (L, MLA_KV_RANK, MLA_HEADS * (MLA_NOPE + MLA_V)), MLA_KV_RANK ** -0.5),
        "w_oa": nrm(ks[17], (L, MLA_HEADS * MLA_V, D_MODEL), (MLA_HEADS * MLA_V) ** -0.5),
        "w_ob": nrm(ks[18], (L, FOX_WIDTH, D_MODEL), FOX_WIDTH ** -0.5),
        "w_o": nrm(ks[19], (L, D_MODEL, D_MODEL), D_MODEL ** -0.5),
        "g_ffn": gain(ks[20], D_MODEL),
        "w_up": nrm(ks[21], (L, D_MODEL, 2 * D_FF), D_MODEL ** -0.5),
        "w_conv": nrm(ks[22], (L, CONV_W, 2 * D_FF), CONV_W ** -0.5),
        "b_conv": nrm(ks[23], (L, 2 * D_FF), 0.01),
        "w_down": nrm(ks[24], (L, D_FF, D_MODEL), D_FF ** -0.5),
        "g_ple": gain(ks[25], D_MODEL),
        "w_pg": nrm(ks[26], (L, D_MODEL, D_MODEL), D_MODEL ** -0.5),
        "w_ple": nrm(ks[27], (L, PLE_DIM, D_MODEL), PLE_DIM ** -0.5),
        "g_final": 1.0 + 0.1 * jax.random.normal(ks[28], (D_MODEL,), jnp.float32),
    }


def reference(x_prompt, x_sample, cache_mla_ckv, cache_mla_krope, cache_fox_k, cache_fox_v,
              cache_fox_logf, state_ffn_conv, p_prompt, p_sample, g_mix, w_in, b_f, g_q, w_uq,
              g_kv, w_ukv, w_oa, w_ob, w_o, g_ffn, w_up, w_conv, b_conv, w_down, g_ple, w_pg,
              w_ple, g_final):
    def layer_weights(i):
        return {"g_mix": g_mix[i], "w_in": w_in[i], "b_f": b_f[i], "g_q": g_q[i], "w_uq": w_uq[i],
                "g_kv": g_kv[i], "w_ukv": w_ukv[i], "w_oa": w_oa[i], "w_ob": w_ob[i], "w_o": w_o[i],
                "g_ffn": g_ffn[i], "w_up": w_up[i], "w_conv": w_conv[i], "b_conv": b_conv[i],
                "w_down": w_down[i], "g_ple": g_ple[i], "w_pg": w_pg[i], "w_ple": w_ple[i]}

    h_p, st_p = x_prompt, []
    for i in range(DEPTH):
        h_p, st = layer(h_p, p_prompt[i], None, layer_weights(i))
        st_p.append(st)
    y_prompt = rmsnorm(h_p, g_final)

    h_s, st_s = x_sample, []
    for i in range(DEPTH):
        past = (cache_mla_ckv[i], cache_mla_krope[i], cache_fox_k[i], cache_fox_v[i],
                cache_fox_logf[i], state_ffn_conv[i])
        h_s, st = layer(h_s, p_sample[i], past, layer_weights(i))
        st_s.append(st)
    y_sample = rmsnorm(h_s, g_final)

    def stk(states, j):
        return jnp.stack([s[j] for s in states], axis=0)

    return (y_prompt, y_sample,
            stk(st_p, 0), stk(st_s, 0),
            stk(st_p, 1), stk(st_s, 1),
            stk(st_p, 2), stk(st_s, 2),
            stk(st_p, 3), stk(st_s, 3),
            stk(st_p, 4), stk(st_s, 4),
            stk(st_p, 5), stk(st_s, 5))
```

```python
import functools
import math

import jax
import jax.numpy as jnp
from jax import lax
from jax.experimental import pallas as pl
from jax.experimental.pallas import tpu as pltpu

F32 = jnp.float32
BF16 = jnp.bfloat16

CHUNK = 64
MLA_HEADS = 8
MLA_Q_RANK = 512
MLA_KV_RANK = 512
MLA_NOPE = 128
MLA_ROPE = 64
MLA_V = 128
MLA_SCALE = (MLA_NOPE + MLA_ROPE) ** -0.5
ROPE_THETA = 10000.0
FOX_HEADS = 8
FOX_HEAD_DIM = 128
FOX_WIDTH = FOX_HEADS * FOX_HEAD_DIM
FOX_SCALE = FOX_HEAD_DIM ** -0.5
CONV_W = 3
EPS = 1e-6
NEG_INF = -1e30
LOG2E = 1.4426950408889634

HEADS = 8
HEAD_DIM = 128
QK_WIDTH = 256
LANES = 128
V7X_VMEM_LIMIT = 56 * 1024 * 1024

assert MLA_HEADS == HEADS and FOX_HEADS == HEADS
assert MLA_NOPE == HEAD_DIM and MLA_V == HEAD_DIM and FOX_HEAD_DIM == HEAD_DIM


def _tile(n, pref):
    if n <= pref:
        return n
    for t in range(pref, 7, -1):
        if n % t == 0 and t % 8 == 0:
            return t
    return n


def _rms(x, g):
    ms = jnp.mean(x * x, axis=-1, keepdims=True)
    return x * lax.rsqrt(ms + EPS) * g


def _split3(x):
    a1 = x.astype(BF16).astype(F32)
    r1 = x - a1
    a2 = r1.astype(BF16).astype(F32)
    a3 = (r1 - a2).astype(BF16).astype(F32)
    return a1, a2, a3


def _params(n_axes, vmem=None):
    return pltpu.CompilerParams(
        dimension_semantics=("arbitrary",) * n_axes,
        vmem_limit_bytes=vmem,
    )


def _in_proj_kernel(x_ref, g_ref, w_ref, ws_ref, a_ref, s_ref, kv_ref, go_ref, xn_ref,
                    *, n_a, n_kv):
    j = pl.program_id(1)

    @pl.when(j == 0)
    def _():
        xn = _rms(x_ref[...], g_ref[...]).astype(BF16)
        xn_ref[...] = xn
        s_ref[...] = jnp.dot(xn, ws_ref[...], preferred_element_type=F32)

    def mm():
        return jnp.dot(xn_ref[...], w_ref[...], preferred_element_type=F32)

    @pl.when(j < n_a)
    def _():
        a_ref[...] = mm()

    @pl.when(jnp.logical_and(j >= n_a, j < n_a + n_kv))
    def _():
        kv_ref[...] = mm()

    @pl.when(j >= n_a + n_kv)
    def _():
        go_ref[...] = mm().astype(BF16)


def _in_proj(x2d, g, w_all, w_s, d_model):
    n, d = x2d.shape
    tm = _tile(n, 512)
    tn = math.gcd(1024, 2 * d_model)
    wa, wkv, wg = 2 * MLA_Q_RANK, 2 * FOX_WIDTH, 2 * d_model + FOX_WIDTH
    n_a, n_kv, n_g = wa // tn, wkv // tn, wg // tn
    ws = w_s.shape[1]
    kern = functools.partial(_in_proj_kernel, n_a=n_a, n_kv=n_kv)
    return pl.pallas_call(
        kern,
        grid=(n // tm, n_a + n_kv + n_g),
        in_specs=[
            pl.BlockSpec((tm, d), lambda i, j: (i, 0)),
            pl.BlockSpec((1, d), lambda i, j: (0, 0)),
            pl.BlockSpec((d, tn), lambda i, j: (0, j)),
            pl.BlockSpec((d, ws), lambda i, j: (0, 0)),
        ],
        out_specs=[
            pl.BlockSpec((tm, tn), lambda i, j: (i, jnp.minimum(j, n_a - 1))),
            pl.BlockSpec((tm, ws), lambda i, j: (i, 0)),
            pl.BlockSpec((tm, tn), lambda i, j: (i, jnp.clip(j - n_a, 0, n_kv - 1))),
            pl.BlockSpec((tm, tn), lambda i, j: (i, jnp.clip(j - n_a - n_kv, 0, n_g - 1))),
        ],
        out_shape=[
            jax.ShapeDtypeStruct((n, wa), F32),
            jax.ShapeDtypeStruct((n, ws), F32),
            jax.ShapeDtypeStruct((n, wkv), F32),
            jax.ShapeDtypeStruct((n, wg), BF16),
        ],
        scratch_shapes=[pltpu.VMEM((tm, d), BF16)],
        compiler_params=_params(2, V7X_VMEM_LIMIT),
        name="in_proj",
    )(x2d, g, w_all, w_s)


def _kv_up(ckvn, kr128, wukv_ref, kc_ref, v_ref):
    kv = jnp.dot(ckvn.astype(BF16), wukv_ref[...], preferred_element_type=F32)
    krb = kr128.astype(BF16)
    for h in range(HEADS):
        kc_ref[h, :, 0:HEAD_DIM] = kv[:, h * HEAD_DIM:(h + 1) * HEAD_DIM].astype(BF16)
        kc_ref[h, :, HEAD_DIM:QK_WIDTH] = krb
        v_ref[h] = kv[:, (HEADS + h) * HEAD_DIM:(HEADS + h + 1) * HEAD_DIM].astype(BF16)


def _mla_pre_kernel(a_ref, sm_ref, c_ref, s_ref, gq_ref, gkv_ref, bf_ref, wuq_ref, wukv_ref,
                    q_ref, ckv_ref, kr_ref, lf8_ref, lfp_ref, kc_ref, v_ref):
    a = a_ref[...]
    cos = c_ref[...]
    sin = s_ref[...]
    qn = _rms(a[:, :MLA_Q_RANK], gq_ref[...]).astype(BF16)
    q3 = jnp.dot(qn, wuq_ref[...], preferred_element_type=F32)
    hw = HEADS * HEAD_DIM
    for h in range(HEADS):
        lo, hi = h * HEAD_DIM, (h + 1) * HEAD_DIM
        q_ref[h, :, 0:HEAD_DIM] = q3[:, lo:hi].astype(BF16)
        rot = q3[:, hw + lo:hw + hi] * cos + q3[:, 2 * hw + lo:2 * hw + hi] * sin
        q_ref[h, :, HEAD_DIM:QK_WIDTH] = rot.astype(BF16)

    ckvn = _rms(a[:, MLA_Q_RANK:], gkv_ref[...])
    ckv_ref[...] = ckvn
    sm = sm_ref[...]
    kr128 = sm[:, 0:LANES] * cos + sm[:, LANES:2 * LANES] * sin
    kr_ref[...] = kr128[:, :MLA_ROPE]
    z = sm[:, 2 * LANES:3 * LANES] + bf_ref[...]
    lf = jnp.minimum(z, 0.0) - jnp.log1p(jnp.exp(-jnp.abs(z)))
    lane = lax.broadcasted_iota(jnp.int32, lf.shape, 1)
    lfp_ref[...] = jnp.where(lane < FOX_HEADS, lf, 0.0)
    lf8_ref[...] = lf[:, :FOX_HEADS]
    _kv_up(ckvn, kr128, wukv_ref, kc_ref, v_ref)


def _mla_pre(a3, sm3, cos_t, sin_t, gq, gkv, bf_pad, wuq3, wukv2):
    b, t, wa = a3.shape
    tm = _tile(t, 256)
    ws = sm3.shape[2]
    const = lambda bb, i: (0, 0)
    row3 = lambda bb, i: (bb, i, 0)
    head4 = lambda bb, i: (bb, 0, i, 0)
    return pl.pallas_call(
        _mla_pre_kernel,
        grid=(b, t // tm),
        in_specs=[
            pl.BlockSpec((None, tm, wa), row3),
            pl.BlockSpec((None, tm, ws), row3),
            pl.BlockSpec((tm, LANES), lambda bb, i: (i, 0)),
            pl.BlockSpec((tm, LANES), lambda bb, i: (i, 0)),
            pl.BlockSpec((1, MLA_Q_RANK), const),
            pl.BlockSpec((1, MLA_KV_RANK), const),
            pl.BlockSpec((1, LANES), const),
            pl.BlockSpec(wuq3.shape, const),
            pl.BlockSpec(wukv2.shape, const),
        ],
        out_specs=[
            pl.BlockSpec((None, HEADS, tm, QK_WIDTH), head4),
            pl.BlockSpec((None, tm, MLA_KV_RANK), row3),
            pl.BlockSpec((None, tm, MLA_ROPE), row3),
            pl.BlockSpec((None, tm, FOX_HEADS), row3),
            pl.BlockSpec((None, tm, LANES), row3),
            pl.BlockSpec((None, HEADS, tm, QK_WIDTH), head4),
            pl.BlockSpec((None, HEADS, tm, HEAD_DIM), head4),
        ],
        out_shape=[
            jax.ShapeDtypeStruct((b, HEADS, t, QK_WIDTH), BF16),
            jax.ShapeDtypeStruct((b, t, MLA_KV_RANK), F32),
            jax.ShapeDtypeStruct((b, t, MLA_ROPE), F32),
            jax.ShapeDtypeStruct((b, t, FOX_HEADS), F32),
            jax.ShapeDtypeStruct((b, t, LANES), F32),
            jax.ShapeDtypeStruct((b, HEADS, t, QK_WIDTH), BF16),
            jax.ShapeDtypeStruct((b, HEADS, t, HEAD_DIM), BF16),
        ],
        compiler_params=_params(2, V7X_VMEM_LIMIT),
        name="mla_pre",
    )(a3, sm3, cos_t, sin_t, gq, gkv, bf_pad, wuq3, wukv2)


def _kv_up_kernel(ckv_ref, kr_ref, wukv_ref, kc_ref, v_ref):
    kr = kr_ref[...]
    kr128 = jnp.concatenate([kr, jnp.zeros((kr.shape[0], LANES - MLA_ROPE), F32)], axis=1)
    _kv_up(ckv_ref[...], kr128, wukv_ref, kc_ref, v_ref)


def _kv_up_call(ckv3, kr3, wukv2):
    b, s, _ = ckv3.shape
    ts = _tile(s, 512)
    row3 = lambda bb, i: (bb, i, 0)
    head4 = lambda bb, i: (bb, 0, i, 0)
    return pl.pallas_call(
        _kv_up_kernel,
        grid=(b, s // ts),
        in_specs=[
            pl.BlockSpec((None, ts, MLA_KV_RANK), row3),
            pl.BlockSpec((None, ts, MLA_ROPE), row3),
            pl.BlockSpec(wukv2.shape, lambda bb, i: (0, 0)),
        ],
        out_specs=[
            pl.BlockSpec((None, HEADS, ts, QK_WIDTH), head4),
            pl.BlockSpec((None, HEADS, ts, HEAD_DIM), head4),
        ],
        out_shape=[
            jax.ShapeDtypeStruct((b, HEADS, s, QK_WIDTH), BF16),
            jax.ShapeDtypeStruct((b, HEADS, s, HEAD_DIM), BF16),
        ],
        compiler_params=_params(2, V7X_VMEM_LIMIT),
        name="kv_up",
    )(ckv3, kr3, wukv2)


def _fox_prep_kernel(*refs, has_q):
    if has_q:
        (q_ref, k_ref, v_ref, lf_ref, init_ref,
         qo_ref, ko_ref, vo_ref, last_ref, carry_ref) = refs
    else:
        (k_ref, v_ref, lf_ref, init_ref, ko_ref, vo_ref, last_ref, carry_ref) = refs
    i = pl.program_id(1)

    @pl.when(i == 0)
    def _():
        carry_ref[...] = init_ref[...]

    lf = lf_ref[...]
    ts = lf.shape[0]
    row = lax.broadcasted_iota(jnp.int32, (ts, ts), 0)
    col = lax.broadcasted_iota(jnp.int32, (ts, ts), 1)
    tri = jnp.where(col <= row, 1.0, 0.0).astype(BF16)
    cum = carry_ref[...]
    for part in _split3(lf):
        cum = cum + jnp.dot(tri, part.astype(BF16), preferred_element_type=F32)
    carry_ref[...] = cum[ts - 1:ts, :]
    last_ref[...] = cum[ts - 1:ts, :]

    c1, c2, c3 = _split3(cum * LOG2E)
    lane = lax.broadcasted_iota(jnp.int32, (ts, LANES), 1)
    k = k_ref[...]
    v = v_ref[...]
    for h in range(HEADS):
        lo, hi = h * HEAD_DIM, (h + 1) * HEAD_DIM
        h1, h2, h3 = c1[:, h:h + 1], c2[:, h:h + 1], c3[:, h:h + 1]
        ek = jnp.where(lane < 3, 1.0,
                       jnp.where(lane == 3, -h1,
                                 jnp.where(lane == 4, -h2,
                                           jnp.where(lane == 5, -h3, 0.0))))
        ko_ref[h, :, 0:HEAD_DIM] = k[:, lo:hi].astype(BF16)
        ko_ref[h, :, HEAD_DIM:QK_WIDTH] = ek.astype(BF16)
        vo_ref[h] = v[:, lo:hi].astype(BF16)
        if has_q:
            eq = jnp.where(lane == 0, h1,
                           jnp.where(lane == 1, h2,
                                     jnp.where(lane == 2, h3,
                                               jnp.where(lane < 6, 1.0, 0.0))))
            qo_ref[h, :, 0:HEAD_DIM] = q_ref[:, lo:hi]
            qo_ref[h, :, HEAD_DIM:QK_WIDTH] = eq.astype(BF16)


def _fox_prep(q_src, k_src, v_src, lf_pad, init):
    k_arr, k_blk = k_src
    v_arr, v_blk = v_src
    b, s, _ = k_arr.shape
    ts = _tile(s, 256)
    has_q = q_src is not None
    head4 = lambda bb, i: (bb, 0, i, 0)
    in_specs, args = [], []
    if has_q:
        q_arr, q_blk = q_src
        in_specs.append(pl.BlockSpec((None, ts, FOX_WIDTH), lambda bb, i: (bb, i, q_blk)))
        args.append(q_arr)
    in_specs += [
        pl.BlockSpec((None, ts, FOX_WIDTH), lambda bb, i: (bb, i, k_blk)),
        pl.BlockSpec((None, ts, FOX_WIDTH), lambda bb, i: (bb, i, v_blk)),
        pl.BlockSpec((None, ts, LANES), lambda bb, i: (bb, i, 0)),
        pl.BlockSpec((None, 1, LANES), lambda bb, i: (bb, 0, 0)),
    ]
    args += [k_arr, v_arr, lf_pad, init]
    out_specs, out_shape = [], []
    if has_q:
        out_specs.append(pl.BlockSpec((None, HEADS, ts, QK_WIDTH), head4))
        out_shape.append(jax.ShapeDtypeStruct((b, HEADS, s, QK_WIDTH), BF16))
    out_specs += [
        pl.BlockSpec((None, HEADS, ts, QK_WIDTH), head4),
        pl.BlockSpec((None, HEADS, ts, HEAD_DIM), head4),
        pl.BlockSpec((None, 1, LANES), lambda bb, i: (bb, 0, 0)),
    ]
    out_shape += [
        jax.ShapeDtypeStruct((b, HEADS, s, QK_WIDTH), BF16),
        jax.ShapeDtypeStruct((b, HEADS, s, HEAD_DIM), BF16),
        jax.ShapeDtypeStruct((b, 1, LANES), F32),
    ]
    return pl.pallas_call(
        functools.partial(_fox_prep_kernel, has_q=has_q),
        grid=(b, s // ts),
        in_specs=in_specs,
        out_specs=out_specs,
        out_shape=out_shape,
        scratch_shapes=[pltpu.VMEM((1, LANES), F32)],
        compiler_params=_params(2, V7X_VMEM_LIMIT),
        name="fox_prep_q" if has_q else "fox_prep_past",
    )(*args)


def _attn_kernel(*refs, tq, tkp, n_past, mask_shift):
    if n_past:
        q_ref, kn_ref, vn_ref, kp_ref, vp_ref, o_ref, m_ref, l_ref, acc_ref = refs
    else:
        q_ref, kn_ref, vn_ref, o_ref, m_ref, l_ref, acc_ref = refs
    i = pl.program_id(2)
    m_ref[...] = jnp.full(m_ref.shape, NEG_INF, F32)
    l_ref[...] = jnp.zeros(l_ref.shape, F32)
    acc_ref[...] = jnp.zeros(acc_ref.shape, F32)
    q = q_ref[...]

    def update(k_blk, v_blk, mask):
        s = lax.dot_general(q, k_blk, (((1,), (1,)), ((), ())), preferred_element_type=F32)
        if mask is not None:
            s = jnp.where(mask, s, NEG_INF)
        m_prev = m_ref[...]
        m_new = jnp.maximum(m_prev, jnp.max(s, axis=1, keepdims=True))
        alpha = jnp.exp2(m_prev - m_new)
        p = jnp.exp2(s - m_new[:, 0:1])
        l_ref[...] = alpha * l_ref[...] + jnp.sum(p, axis=1, keepdims=True)
        acc_ref[...] = alpha * acc_ref[...] + jnp.dot(
            p.astype(BF16), v_blk, preferred_element_type=F32)
        m_ref[...] = m_new

    if n_past:
        def past_body(j, c):
            off = pl.multiple_of(j * tkp, tkp)
            update(kp_ref[pl.ds(off, tkp), :], vp_ref[pl.ds(off, tkp), :], None)
            return c
        lax.fori_loop(0, n_past, past_body, 0)

    def new_body(j, c):
        off = pl.multiple_of(j * tq, tq)
        update(kn_ref[pl.ds(off, tq), :], vn_ref[pl.ds(off, tq), :], None)
        return c
    lax.fori_loop(0, i, new_body, 0)

    off = pl.multiple_of(i * tq, tq)
    row = lax.broadcasted_iota(jnp.int32, (tq, tq), 0)
    col = lax.broadcasted_iota(jnp.int32, (tq, tq), 1)
    mask = (col >> mask_shift) <= (row >> mask_shift)
    update(kn_ref[pl.ds(off, tq), :], vn_ref[pl.ds(off, tq), :], mask)
    o_ref[...] = (acc_ref[...] / l_ref[...]).astype(o_ref.dtype)


def _attention(q4, kn4, vn4, past, mask_shift):
    b, h, t, _ = q4.shape
    tq = _tile(t, 512)
    assert tq % (1 << mask_shift) == 0 or t == tq
    in_specs = [
        pl.BlockSpec((None, None, tq, QK_WIDTH), lambda bb, hh, i: (bb, hh, i, 0)),
        pl.BlockSpec((None, None, t, QK_WIDTH), lambda bb, hh, i: (bb, hh, 0, 0)),
        pl.BlockSpec((None, None, t, HEAD_DIM), lambda bb, hh, i: (bb, hh, 0, 0)),
    ]
    args = [q4, kn4, vn4]
    n_past, tkp = 0, 0
    if past is not None:
        kp4, vp4 = past
        p_len = kp4.shape[2]
        tkp = _tile(p_len, 512)
        n_past = p_len // tkp
        in_specs += [
            pl.BlockSpec((None, None, p_len, QK_WIDTH), lambda bb, hh, i: (bb, hh, 0, 0)),
            pl.BlockSpec((None, None, p_len, HEAD_DIM), lambda bb, hh, i: (bb, hh, 0, 0)),
        ]
        args += [kp4, vp4]
    kern = functools.partial(_attn_kernel, tq=tq, tkp=tkp, n_past=n_past, mask_shift=mask_shift)
    return pl.pallas_call(
        kern,
        grid=(b, h, t // tq),
        in_specs=in_specs,
        out_specs=pl.BlockSpec((None, tq, HEAD_DIM), lambda bb, hh, i: (bb, i, hh)),
        out_shape=jax.ShapeDtypeStruct((b, t, h * HEAD_DIM), BF16),
        scratch_shapes=[pltpu.VMEM((tq, LANES), F32), pltpu.VMEM((tq, LANES), F32),
                        pltpu.VMEM((tq, HEAD_DIM), F32)],
        compiler_params=_params(3, V7X_VMEM_LIMIT),
        name="attn_chunk" if mask_shift else "attn_frame",
    )(*args)


def _out_merge_kernel(oa_ref, ob_ref, ga_ref, gb_ref, x_ref, woa_ref, wob_ref, wo_ref,
                      h_ref, mg_ref, *, tc):
    d = x_ref.shape[1]
    oa = oa_ref[...]
    ob = ob_ref[...]
    for c in range(d // tc):
        sl = slice(c * tc, (c + 1) * tc)
        ta = jnp.dot(oa, woa_ref[:, sl], preferred_element_type=F32)
        tb = jnp.dot(ob, wob_ref[:, sl], preferred_element_type=F32)
        ga = jax.nn.sigmoid(ga_ref[:, sl].astype(F32))
        gb = jax.nn.sigmoid(gb_ref[:, sl].astype(F32))
        mg_ref[:, sl] = (ga * ta + gb * tb).astype(BF16)
    mg = mg_ref[...]
    for c in range(d // tc):
        sl = slice(c * tc, (c + 1) * tc)
        h_ref[:, sl] = x_ref[:, sl] + jnp.dot(mg, wo_ref[:, sl], preferred_element_type=F32)


def _out_merge(oa, ob, g_arr, x2d, woa, wob, wo):
    n, d = x2d.shape
    tm = _tile(n, 256)
    tc = _tile(d, 512)
    const = lambda i: (0, 0)
    one = pl.Buffered(1)
    return pl.pallas_call(
        functools.partial(_out_merge_kernel, tc=tc),
        grid=(n // tm,),
        in_specs=[
            pl.BlockSpec((tm, oa.shape[1]), lambda i: (i, 0)),
            pl.BlockSpec((tm, ob.shape[1]), lambda i: (i, 0)),
            pl.BlockSpec((tm, d), lambda i: (i, 0)),
            pl.BlockSpec((tm, d), lambda i: (i, 1)),
            pl.BlockSpec((tm, d), lambda i: (i, 0)),
            pl.BlockSpec(woa.shape, const, pipeline_mode=one),
            pl.BlockSpec(wob.shape, const, pipeline_mode=one),
            pl.BlockSpec(wo.shape, const, pipeline_mode=one),
        ],
        out_specs=pl.BlockSpec((tm, d), lambda i: (i, 0)),
        out_shape=jax.ShapeDtypeStruct((n, d), F32),
        scratch_shapes=[pltpu.VMEM((tm, d), BF16)],
        compiler_params=_params(1, V7X_VMEM_LIMIT),
        name="out_merge",
    )(oa, ob, g_arr, g_arr, x2d, woa, wob, wo)


def _norm_matmul_kernel(x_ref, g_ref, w_ref, o_ref, xn_ref):
    @pl.when(pl.program_id(1) == 0)
    def _():
        xn_ref[...] = _rms(x_ref[...], g_ref[...]).astype(BF16)

    o_ref[...] = jnp.dot(xn_ref[...], w_ref[...], preferred_element_type=F32).astype(o_ref.dtype)


def _norm_matmul(x2d, g, w, out_dtype):
    n, d = x2d.shape
    m = w.shape[1]
    tm = _tile(n, 512)
    tn = _tile(m, 1024)
    assert tn % LANES == 0
    return pl.pallas_call(
        _norm_matmul_kernel,
        grid=(n // tm, m // tn),
        in_specs=[
            pl.BlockSpec((tm, d), lambda i, j: (i, 0)),
            pl.BlockSpec((1, d), lambda i, j: (0, 0)),
            pl.BlockSpec((d, tn), lambda i, j: (0, j)),
        ],
        out_specs=pl.BlockSpec((tm, tn), lambda i, j: (i, j)),
        out_shape=jax.ShapeDtypeStruct((n, m), out_dtype),
        scratch_shapes=[pltpu.VMEM((tm, d), BF16)],
        compiler_params=_params(2, V7X_VMEM_LIMIT),
        name="ffn_up",
    )(x2d, g, w)


def _ffn_down_kernel(uv_ref, ug_ref, hv_ref, hg_ref, wcv_ref, wcg_ref, bcv_ref, bcg_ref,
                     wd_ref, h_ref, o_ref, acc_ref):
    j = pl.program_id(1)

    @pl.when(j == 0)
    def _():
        acc_ref[...] = jnp.zeros(acc_ref.shape, F32)

    def conv(u_ref, halo_ref, wc_ref, bc_ref):
        u = u_ref[...].astype(F32)
        halo = halo_ref[...]
        row = lax.broadcasted_iota(jnp.int32, u.shape, 0)
        u1 = jnp.where(row == 0, halo[1:2, :], pltpu.roll(u, 1, axis=0))
        u2 = jnp.where(row == 0, halo[0:1, :],
                       jnp.where(row == 1, halo[1:2, :], pltpu.roll(u, 2, axis=0)))
        wc = wc_ref[...]
        return bc_ref[...] + u2 * wc[0:1, :] + u1 * wc[1:2, :] + u * wc[2:3, :]

    val = conv(uv_ref, hv_ref, wcv_ref, bcv_ref)
    gate = conv(ug_ref, hg_ref, wcg_ref, bcg_ref)
    cdf = 0.5 * (1.0 + jnp.tanh(0.7978845608028654 * (gate + 0.044715 * (gate * gate * gate))))
    act = (gate * cdf * val).astype(BF16)
    acc_ref[...] += jnp.dot(act, wd_ref[...], preferred_element_type=F32)

    @pl.when(j == pl.num_programs(1) - 1)
    def _():
        o_ref[...] = h_ref[...] + acc_ref[...]


def _ffn_down(u2d, halo, w_conv, b_conv, w_down, h2d, t):
    n, d = h2d.shape
    f = w_down.shape[0]
    tm = _tile(t, 512)
    tc = _tile(f, 512)
    nf = f // tc
    return pl.pallas_call(
        _ffn_down_kernel,
        grid=(n // tm, nf),
        in_specs=[
            pl.BlockSpec((tm, tc), lambda i, j: (i, j)),
            pl.BlockSpec((tm, tc), lambda i, j: (i, j + nf)),
            pl.BlockSpec((None, CONV_W - 1, tc), lambda i, j: (i, 0, j)),
            pl.BlockSpec((None, CONV_W - 1, tc), lambda i, j: (i, 0, j + nf)),
            pl.BlockSpec((CONV_W, tc), lambda i, j: (0, j)),
            pl.BlockSpec((CONV_W, tc), lambda i, j: (0, j + nf)),
            pl.BlockSpec((1, tc), lambda i, j: (0, j)),
            pl.BlockSpec((1, tc), lambda i, j: (0, j + nf)),
            pl.BlockSpec((tc, d), lambda i, j: (j, 0)),
            pl.BlockSpec((tm, d), lambda i, j: (i, 0)),
        ],
        out_specs=pl.BlockSpec((tm, d), lambda i, j: (i, 0)),
        out_shape=jax.ShapeDtypeStruct((n, d), F32),
        scratch_shapes=[pltpu.VMEM((tm, d), F32)],
        compiler_params=_params(2, V7X_VMEM_LIMIT),
        name="ffn_down",
    )(u2d, u2d, halo, halo, w_conv, w_conv, b_conv, b_conv, w_down, h2d)


def _ple_final_kernel(h_ref, pe_ref, gp_ref, gf_ref, wpg_ref, wple_ref, y_ref, h3_ref, *, tc):
    d = h_ref.shape[1]
    hn = _rms(h_ref[...], gp_ref[...]).astype(BF16)
    pe = pe_ref[...].astype(BF16)
    for c in range(d // tc):
        sl = slice(c * tc, (c + 1) * tc)
        gate = jax.nn.sigmoid(jnp.dot(hn, wpg_ref[:, sl], preferred_element_type=F32))
        val = jnp.dot(pe, wple_ref[:, sl], preferred_element_type=F32)
        h3_ref[:, sl] = h_ref[:, sl] + gate * val
    y_ref[...] = _rms(h3_ref[...], gf_ref[...])


def _ple_final(h2d, pe2d, g_ple, g_final, wpg, wple):
    n, d = h2d.shape
    tm = _tile(n, 256)
    tc = _tile(d, 512)
    const = lambda i: (0, 0)
    one = pl.Buffered(1)
    return pl.pallas_call(
        functools.partial(_ple_final_kernel, tc=tc),
        grid=(n // tm,),
        in_specs=[
            pl.BlockSpec((tm, d), lambda i: (i, 0)),
            pl.BlockSpec((tm, pe2d.shape[1]), lambda i: (i, 0)),
            pl.BlockSpec((1, d), const),
            pl.BlockSpec((1, d), const),
            pl.BlockSpec(wpg.shape, const, pipeline_mode=one),
            pl.BlockSpec(wple.shape, const, pipeline_mode=one),
        ],
        out_specs=pl.BlockSpec((tm, d), lambda i: (i, 0)),
        out_shape=jax.ShapeDtypeStruct((n, d), F32),
        scratch_shapes=[pltpu.VMEM((tm, d), F32)],
        compiler_params=_params(1, V7X_VMEM_LIMIT),
        name="ple_final",
    )(h2d, pe2d, g_ple, g_final, wpg, wple)


def _prep_weights(w, d_model):
    half = MLA_ROPE // 2
    o_cq = 0
    o_ckv = o_cq + MLA_Q_RANK
    o_kr = o_ckv + MLA_KV_RANK
    o_fq = o_kr + MLA_ROPE
    o_fk = o_fq + FOX_WIDTH
    o_fv = o_fk + FOX_WIDTH
    o_fl = o_fv + FOX_WIDTH
    o_ga = o_fl + FOX_HEADS
    o_gb = o_ga + d_model
    w_in = w["w_in"]
    d = w_in.shape[0]
    col = lambda o, n: w_in[:, o:o + n]
    w_all = jnp.concatenate([
        col(o_cq, MLA_Q_RANK), col(o_ckv, MLA_KV_RANK),
        col(o_fk, FOX_WIDTH), col(o_fv, FOX_WIDTH),
        col(o_ga, d_model), col(o_gb, d_model),
        col(o_fq, FOX_WIDTH) * (FOX_SCALE * LOG2E),
    ], axis=1).astype(BF16)
    zeros = lambda n: jnp.zeros((d, n), F32)
    w_s = jnp.concatenate([
        col(o_kr, MLA_ROPE), zeros(LANES - MLA_ROPE),
        col(o_kr + half, half), col(o_kr, half), zeros(LANES - MLA_ROPE),
        col(o_fl, FOX_HEADS), zeros(LANES - FOX_HEADS),
    ], axis=1).astype(BF16)

    qk = MLA_NOPE + MLA_ROPE
    wq = w["w_uq"].reshape(MLA_Q_RANK, HEADS, qk) * (MLA_SCALE * LOG2E)
    pad = jnp.zeros((MLA_Q_RANK, HEADS, LANES - MLA_ROPE), F32)
    nope = wq[:, :, :MLA_NOPE]
    x1 = wq[:, :, MLA_NOPE:MLA_NOPE + half]
    x2 = wq[:, :, MLA_NOPE + half:]
    rot = jnp.concatenate([x1, x2, pad], axis=2)
    rot_sw = jnp.concatenate([x2, x1, pad], axis=2)
    wuq3 = jnp.concatenate([nope.reshape(MLA_Q_RANK, -1), rot.reshape(MLA_Q_RANK, -1),
                            rot_sw.reshape(MLA_Q_RANK, -1)], axis=1).astype(BF16)
    wkv = w["w_ukv"].reshape(MLA_KV_RANK, HEADS, MLA_NOPE + MLA_V)
    wukv2 = jnp.concatenate([wkv[:, :, :MLA_NOPE].reshape(MLA_KV_RANK, -1),
                             wkv[:, :, MLA_NOPE:].reshape(MLA_KV_RANK, -1)], axis=1).astype(BF16)
    bf_pad = jnp.zeros((1, LANES), F32).at[0, :FOX_HEADS].set(w["b_f"])
    return dict(
        w_all=w_all, w_s=w_s, wuq3=wuq3, wukv2=wukv2, bf_pad=bf_pad,
        g_mix=w["g_mix"][None, :], g_q=w["g_q"][None, :], g_kv=w["g_kv"][None, :],
        w_oa=w["w_oa"].astype(BF16), w_ob=w["w_ob"].astype(BF16), w_o=w["w_o"].astype(BF16),
        g_ffn=w["g_ffn"][None, :], w_up=w["w_up"].astype(BF16),
        w_conv=w["w_conv"], b_conv=w["b_conv"][None, :], w_down=w["w_down"].astype(BF16),
        g_ple=w["g_ple"][None, :], w_pg=w["w_pg"].astype(BF16), w_ple=w["w_ple"].astype(BF16),
    )


def _rope_tables(pos):
    half = MLA_ROPE // 2
    inv = ROPE_THETA ** (-jnp.arange(half, dtype=F32) / half)
    ang = pos.astype(F32)[:, None] * inv[None, :]
    cos, sin = jnp.cos(ang), jnp.sin(ang)
    zero = jnp.zeros((pos.shape[0], LANES - MLA_ROPE), F32)
    return (jnp.concatenate([cos, cos, zero], axis=1),
            jnp.concatenate([-sin, sin, zero], axis=1))


def _layer(x, pe, past, pw, g_final):
    b, t, d = x.shape
    n = b * t
    p_len = 0 if past is None else past[0].shape[1]
    x2d = x.reshape(n, d)

    a2d, sm2d, kv2d, g2d = _in_proj(x2d, pw["g_mix"], pw["w_all"], pw["w_s"], d)
    cos_t, sin_t = _rope_tables(p_len + jnp.arange(t))
    (q_mla, ckv_n, kr_n, logf, logf_pad, kc_new, v_new) = _mla_pre(
        a2d.reshape(b, t, -1), sm2d.reshape(b, t, -1), cos_t, sin_t,
        pw["g_q"], pw["g_kv"], pw["bf_pad"], pw["wuq3"], pw["wukv2"])

    kv3 = kv2d.reshape(b, t, -1)
    g3 = g2d.reshape(b, t, -1)
    fq_blk = (2 * d) // FOX_WIDTH
    assert (2 * d) % FOX_WIDTH == 0
    if past is None:
        mla_past = None
        fox_past = None
        init = jnp.zeros((b, 1, LANES), F32)
    else:
        c_ckv, c_kr, c_fk, c_fv, c_logf, _ = past
        mla_past = _kv_up_call(c_ckv, c_kr, pw["wukv2"])
        lf_past = jnp.pad(c_logf, ((0, 0), (0, 0), (0, LANES - FOX_HEADS)))
        kp, vp, init = _fox_prep(None, (c_fk.reshape(b, p_len, -1), 0),
                                 (c_fv.reshape(b, p_len, -1), 0), lf_past,
                                 jnp.zeros((b, 1, LANES), F32))
        fox_past = (kp, vp)
    q_fox, k_fox, v_fox, _ = _fox_prep((g3, fq_blk), (kv3, 0), (kv3, 1), logf_pad, init)

    o_a = _attention(q_mla, kc_new, v_new, mla_past, int(math.log2(CHUNK)))
    o_b = _attention(q_fox, k_fox, v_fox, fox_past, 0)

    h2d = _out_merge(o_a.reshape(n, -1), o_b.reshape(n, -1), g2d, x2d,
                     pw["w_oa"], pw["w_ob"], pw["w_o"])

    f2 = pw["w_up"].shape[1]
    u2d = _norm_matmul(h2d, pw["g_ffn"], pw["w_up"], BF16)
    tm = _tile(t, 512)
    nt = t // tm
    prev = (jnp.zeros((b, CONV_W - 1, f2), F32) if past is None else past[5])
    tails = u2d.reshape(b, nt, tm, f2)[:, :nt - 1, tm - (CONV_W - 1):, :].astype(F32)
    halo = jnp.concatenate([prev[:, None], tails], axis=1).reshape(b * nt, CONV_W - 1, f2)
    h_last = h2d.reshape(b, t, d)[:, t - (CONV_W - 1):, :].reshape(b * (CONV_W - 1), d)
    new_conv = _norm_matmul(h_last, pw["g_ffn"], pw["w_up"], F32).reshape(b, CONV_W - 1, f2)
    h2d = _ffn_down(u2d, halo, pw["w_conv"], pw["b_conv"], pw["w_down"], h2d, t)

    y2d = _ple_final(h2d, pe.reshape(n, -1), pw["g_ple"], g_final[None, :], pw["w_pg"], pw["w_ple"])

    fk = kv3[:, :, :FOX_WIDTH].reshape(b, t, FOX_HEADS, FOX_HEAD_DIM)
    fv = kv3[:, :, FOX_WIDTH:].reshape(b, t, FOX_HEADS, FOX_HEAD_DIM)
    return y2d.reshape(b, t, d), (ckv_n, kr_n, fk, fv, logf, new_conv)


def kernel(x_prompt, x_sample, cache_mla_ckv, cache_mla_krope, cache_fox_k, cache_fox_v,
           cache_fox_logf, state_ffn_conv, p_prompt, p_sample, g_mix, w_in, b_f, g_q, w_uq,
           g_kv, w_ukv, w_oa, w_ob, w_o, g_ffn, w_up, w_conv, b_conv, w_down, g_ple, w_pg,
           w_ple, g_final):
    depth = w_in.shape[0]
    assert depth == 1, "the final norm is fused into the layer's last kernel"
    d_model = x_prompt.shape[-1]
    w = {"g_mix": g_mix[0], "w_in": w_in[0], "b_f": b_f[0], "g_q": g_q[0], "w_uq": w_uq[0],
         "g_kv": g_kv[0], "w_ukv": w_ukv[0], "w_oa": w_oa[0], "w_ob": w_ob[0], "w_o": w_o[0],
         "g_ffn": g_ffn[0], "w_up": w_up[0], "w_conv": w_conv[0], "b_conv": b_conv[0],
         "w_down": w_down[0], "g_ple": g_ple[0], "w_pg": w_pg[0], "w_ple": w_ple[0]}
    pw = _prep_weights(w, d_model)

    y_p, st_p = _layer(x_prompt, p_prompt[0], None, pw, g_final)
    past = (cache_mla_ckv[0], cache_mla_krope[0], cache_fox_k[0], cache_fox_v[0],
            cache_fox_logf[0], state_ffn_conv[0])
    y_s, st_s = _layer(x_sample, p_sample[0], past, pw, g_final)

    outs = [y_p, y_s]
    for j in range(6):
        outs.append(st_p[j][None])
        outs.append(st_s[j][None])
    return tuple(outs)
```

```python
import functools
import math

import jax
import jax.numpy as jnp
from jax import lax
from jax.experimental import pallas as pl
from jax.experimental.pallas import tpu as pltpu

F32 = jnp.float32
BF16 = jnp.bfloat16

CHUNK = 64
MLA_HEADS = 8
MLA_Q_RANK = 512
MLA_KV_RANK = 512
MLA_NOPE = 128
MLA_ROPE = 64
MLA_V = 128
MLA_SCALE = (MLA_NOPE + MLA_ROPE) ** -0.5
ROPE_THETA = 10000.0
FOX_HEADS = 8
FOX_HEAD_DIM = 128
FOX_WIDTH = FOX_HEADS * FOX_HEAD_DIM
FOX_SCALE = FOX_HEAD_DIM ** -0.5
CONV_W = 3
EPS = 1e-6
NEG_INF = -1e30
LOG2E = 1.4426950408889634

HEADS = 8
HEAD_DIM = 128
QK_WIDTH = 256
LANES = 128
ATTN_HEADS_PER_STEP = 4
V7X_VMEM_LIMIT = 56 * 1024 * 1024

assert MLA_HEADS == HEADS and FOX_HEADS == HEADS
assert MLA_NOPE == HEAD_DIM and MLA_V == HEAD_DIM and FOX_HEAD_DIM == HEAD_DIM


def _tile(n, pref):
    if n <= pref:
        return n
    for t in range(pref, 7, -1):
        if n % t == 0 and t % 8 == 0:
            return t
    return n


def _rms(x, g):
    ms = jnp.mean(x * x, axis=-1, keepdims=True)
    return x * lax.rsqrt(ms + EPS) * g


def _split3(x):
    a1 = x.astype(BF16).astype(F32)
    r1 = x - a1
    a2 = r1.astype(BF16).astype(F32)
    a3 = (r1 - a2).astype(BF16).astype(F32)
    return a1, a2, a3


def _attn_tile(t):
    return _tile(t, 512)


def _transpose_rows(x):
    r = x.shape[0]
    if r % LANES:
        x = jnp.concatenate([x, jnp.zeros((LANES - r % LANES, x.shape[1]), x.dtype)], axis=0)
    return x.T[:, :r]


def _transpose_cols(x):
    c = x.shape[1]
    if c % LANES:
        x = jnp.concatenate([x, jnp.zeros((x.shape[0], LANES - c % LANES), x.dtype)], axis=1)
    return x.T[:c, :]


def _params(n_axes, vmem=None):
    return pltpu.CompilerParams(
        dimension_semantics=("arbitrary",) * n_axes,
        vmem_limit_bytes=vmem,
    )


def _in_proj_kernel(x_ref, g_ref, w_ref, ws_ref, a_ref, s_ref, kv_ref, go_ref, xn_ref,
                    *, n_a, n_kv):
    j = pl.program_id(1)

    @pl.when(j == 0)
    def _():
        xn = _rms(x_ref[...], g_ref[...]).astype(BF16)
        xn_ref[...] = xn
        s_ref[...] = jnp.dot(xn, ws_ref[...], preferred_element_type=F32)

    def mm():
        return jnp.dot(xn_ref[...], w_ref[...], preferred_element_type=F32)

    @pl.when(j < n_a)
    def _():
        a_ref[...] = mm()

    @pl.when(jnp.logical_and(j >= n_a, j < n_a + n_kv))
    def _():
        kv_ref[...] = mm()

    @pl.when(j >= n_a + n_kv)
    def _():
        go_ref[...] = mm().astype(BF16)


def _in_proj(x2d, g, w_all, w_s, d_model):
    n, d = x2d.shape
    tm = _tile(n, 512)
    tn = math.gcd(1024, 2 * d_model)
    wa, wkv, wg = 2 * MLA_Q_RANK, 2 * FOX_WIDTH, 2 * d_model + FOX_WIDTH
    n_a, n_kv, n_g = wa // tn, wkv // tn, wg // tn
    ws = w_s.shape[1]
    kern = functools.partial(_in_proj_kernel, n_a=n_a, n_kv=n_kv)
    return pl.pallas_call(
        kern,
        grid=(n // tm, n_a + n_kv + n_g),
        in_specs=[
            pl.BlockSpec((tm, d), lambda i, j: (i, 0)),
            pl.BlockSpec((1, d), lambda i, j: (0, 0)),
            pl.BlockSpec((d, tn), lambda i, j: (0, j)),
            pl.BlockSpec((d, ws), lambda i, j: (0, 0)),
        ],
        out_specs=[
            pl.BlockSpec((tm, tn), lambda i, j: (i, jnp.minimum(j, n_a - 1))),
            pl.BlockSpec((tm, ws), lambda i, j: (i, 0)),
            pl.BlockSpec((tm, tn), lambda i, j: (i, jnp.clip(j - n_a, 0, n_kv - 1))),
            pl.BlockSpec((tm, tn), lambda i, j: (i, jnp.clip(j - n_a - n_kv, 0, n_g - 1))),
        ],
        out_shape=[
            jax.ShapeDtypeStruct((n, wa), F32),
            jax.ShapeDtypeStruct((n, ws), F32),
            jax.ShapeDtypeStruct((n, wkv), F32),
            jax.ShapeDtypeStruct((n, wg), BF16),
        ],
        scratch_shapes=[pltpu.VMEM((tm, d), BF16)],
        compiler_params=_params(2, V7X_VMEM_LIMIT),
        name="in_proj",
    )(x2d, g, w_all, w_s)


def _kv_up(ckvn, kr128, wukv_ref, kc_ref, v_ref):
    kv = jnp.dot(ckvn.astype(BF16), wukv_ref[...], preferred_element_type=F32)
    krb = kr128.astype(BF16)
    for h in range(HEADS):
        kc_ref[h, :, 0:HEAD_DIM] = kv[:, h * HEAD_DIM:(h + 1) * HEAD_DIM].astype(BF16)
        kc_ref[h, :, HEAD_DIM:QK_WIDTH] = krb
        v_ref[h, 0] = _transpose_rows(
            kv[:, (HEADS + h) * HEAD_DIM:(HEADS + h + 1) * HEAD_DIM]).astype(BF16)


def _mla_pre_kernel(a_ref, sm_ref, c_ref, s_ref, gq_ref, gkv_ref, bf_ref, wuq_ref, wukv_ref,
                    q_ref, ckv_ref, kr_ref, lf8_ref, lfp_ref, kc_ref, v_ref):
    a = a_ref[...]
    cos = c_ref[...]
    sin = s_ref[...]
    qn = _rms(a[:, :MLA_Q_RANK], gq_ref[...]).astype(BF16)
    q3 = jnp.dot(qn, wuq_ref[...], preferred_element_type=F32)
    hw = HEADS * HEAD_DIM
    for h in range(HEADS):
        lo, hi = h * HEAD_DIM, (h + 1) * HEAD_DIM
        q_ref[h, :, 0:HEAD_DIM] = q3[:, lo:hi].astype(BF16)
        rot = q3[:, hw + lo:hw + hi] * cos + q3[:, 2 * hw + lo:2 * hw + hi] * sin
        q_ref[h, :, HEAD_DIM:QK_WIDTH] = rot.astype(BF16)

    ckvn = _rms(a[:, MLA_Q_RANK:], gkv_ref[...])
    ckv_ref[...] = ckvn
    sm = sm_ref[...]
    kr128 = sm[:, 0:LANES] * cos + sm[:, LANES:2 * LANES] * sin
    kr_ref[...] = kr128[:, :MLA_ROPE]
    z = sm[:, 2 * LANES:3 * LANES] + bf_ref[...]
    lf = jnp.minimum(z, 0.0) - jnp.log1p(jnp.exp(-jnp.abs(z)))
    lane = lax.broadcasted_iota(jnp.int32, lf.shape, 1)
    lfp_ref[...] = jnp.where(lane < FOX_HEADS, lf, 0.0)
    lf8_ref[...] = lf[:, :FOX_HEADS]
    _kv_up(ckvn, kr128, wukv_ref, kc_ref, v_ref)


def _mla_pre(a3, sm3, cos_t, sin_t, gq, gkv, bf_pad, wuq3, wukv2):
    b, t, wa = a3.shape
    tm = _attn_tile(t)
    ws = sm3.shape[2]
    const = lambda bb, i: (0, 0)
    row3 = lambda bb, i: (bb, i, 0)
    head4 = lambda bb, i: (bb, 0, i, 0)
    vt5 = lambda bb, i: (bb, 0, i, 0, 0)
    return pl.pallas_call(
        _mla_pre_kernel,
        grid=(b, t // tm),
        in_specs=[
            pl.BlockSpec((None, tm, wa), row3),
            pl.BlockSpec((None, tm, ws), row3),
            pl.BlockSpec((tm, LANES), lambda bb, i: (i, 0)),
            pl.BlockSpec((tm, LANES), lambda bb, i: (i, 0)),
            pl.BlockSpec((1, MLA_Q_RANK), const),
            pl.BlockSpec((1, MLA_KV_RANK), const),
            pl.BlockSpec((1, LANES), const),
            pl.BlockSpec(wuq3.shape, const),
            pl.BlockSpec(wukv2.shape, const),
        ],
        out_specs=[
            pl.BlockSpec((None, HEADS, tm, QK_WIDTH), head4),
            pl.BlockSpec((None, tm, MLA_KV_RANK), row3),
            pl.BlockSpec((None, tm, MLA_ROPE), row3),
            pl.BlockSpec((None, tm, FOX_HEADS), row3),
            pl.BlockSpec((None, tm, LANES), row3),
            pl.BlockSpec((None, HEADS, tm, QK_WIDTH), head4),
            pl.BlockSpec((None, HEADS, 1, HEAD_DIM, tm), vt5),
        ],
        out_shape=[
            jax.ShapeDtypeStruct((b, HEADS, t, QK_WIDTH), BF16),
            jax.ShapeDtypeStruct((b, t, MLA_KV_RANK), F32),
            jax.ShapeDtypeStruct((b, t, MLA_ROPE), F32),
            jax.ShapeDtypeStruct((b, t, FOX_HEADS), F32),
            jax.ShapeDtypeStruct((b, t, LANES), F32),
            jax.ShapeDtypeStruct((b, HEADS, t, QK_WIDTH), BF16),
            jax.ShapeDtypeStruct((b, HEADS, t // tm, HEAD_DIM, tm), BF16),
        ],
        compiler_params=_params(2, V7X_VMEM_LIMIT),
        name="mla_pre",
    )(a3, sm3, cos_t, sin_t, gq, gkv, bf_pad, wuq3, wukv2)


def _kv_up_kernel(ckv_ref, kr_ref, wukv_ref, kc_ref, v_ref):
    kr = kr_ref[...]
    kr128 = jnp.concatenate([kr, jnp.zeros((kr.shape[0], LANES - MLA_ROPE), F32)], axis=1)
    _kv_up(ckv_ref[...], kr128, wukv_ref, kc_ref, v_ref)


def _kv_up_call(ckv3, kr3, wukv2):
    b, s, _ = ckv3.shape
    ts = _attn_tile(s)
    row3 = lambda bb, i: (bb, i, 0)
    head4 = lambda bb, i: (bb, 0, i, 0)
    return pl.pallas_call(
        _kv_up_kernel,
        grid=(b, s // ts),
        in_specs=[
            pl.BlockSpec((None, ts, MLA_KV_RANK), row3),
            pl.BlockSpec((None, ts, MLA_ROPE), row3),
            pl.BlockSpec(wukv2.shape, lambda bb, i: (0, 0)),
        ],
        out_specs=[
            pl.BlockSpec((None, HEADS, ts, QK_WIDTH), head4),
            pl.BlockSpec((None, HEADS, 1, HEAD_DIM, ts), lambda bb, i: (bb, 0, i, 0, 0)),
        ],
        out_shape=[
            jax.ShapeDtypeStruct((b, HEADS, s, QK_WIDTH), BF16),
            jax.ShapeDtypeStruct((b, HEADS, s // ts, HEAD_DIM, ts), BF16),
        ],
        compiler_params=_params(2, V7X_VMEM_LIMIT),
        name="kv_up",
    )(ckv3, kr3, wukv2)


def _fox_prep_kernel(*refs, has_q):
    if has_q:
        (q_ref, k_ref, v_ref, lf_ref, init_ref,
         qo_ref, ko_ref, vo_ref, last_ref, carry_ref) = refs
    else:
        (k_ref, v_ref, lf_ref, init_ref, ko_ref, vo_ref, last_ref, carry_ref) = refs
    i = pl.program_id(1)

    @pl.when(i == 0)
    def _():
        carry_ref[...] = init_ref[...]

    lf = lf_ref[...]
    ts = lf.shape[0]
    row = lax.broadcasted_iota(jnp.int32, (ts, ts), 0)
    col = lax.broadcasted_iota(jnp.int32, (ts, ts), 1)
    tri = jnp.where(col <= row, 1.0, 0.0).astype(BF16)
    cum = carry_ref[...]
    for part in _split3(lf):
        cum = cum + jnp.dot(tri, part.astype(BF16), preferred_element_type=F32)
    carry_ref[...] = cum[ts - 1:ts, :]
    last_ref[...] = cum[ts - 1:ts, :]

    c1, c2, c3 = _split3(cum * LOG2E)
    lane = lax.broadcasted_iota(jnp.int32, (ts, LANES), 1)
    k = k_ref[...]
    v = v_ref[...]
    for h in range(HEADS):
        lo, hi = h * HEAD_DIM, (h + 1) * HEAD_DIM
        h1, h2, h3 = c1[:, h:h + 1], c2[:, h:h + 1], c3[:, h:h + 1]
        ek = jnp.where(lane < 3, 1.0,
                       jnp.where(lane == 3, -h1,
                                 jnp.where(lane == 4, -h2,
                                           jnp.where(lane == 5, -h3, 0.0))))
        ko_ref[h, :, 0:HEAD_DIM] = k[:, lo:hi].astype(BF16)
        ko_ref[h, :, HEAD_DIM:QK_WIDTH] = ek.astype(BF16)
        vo_ref[h, 0] = _transpose_rows(v[:, lo:hi]).astype(BF16)
        if has_q:
            eq = jnp.where(lane == 0, h1,
                           jnp.where(lane == 1, h2,
                                     jnp.where(lane == 2, h3,
                                               jnp.where(lane < 6, 1.0, 0.0))))
            qo_ref[h, :, 0:HEAD_DIM] = q_ref[:, lo:hi]
            qo_ref[h, :, HEAD_DIM:QK_WIDTH] = eq.astype(BF16)


def _fox_prep(q_src, k_src, v_src, lf_pad, init):
    k_arr, k_blk = k_src
    v_arr, v_blk = v_src
    b, s, _ = k_arr.shape
    ts = _attn_tile(s)
    has_q = q_src is not None
    head4 = lambda bb, i: (bb, 0, i, 0)
    in_specs, args = [], []
    if has_q:
        q_arr, q_blk = q_src
        in_specs.append(pl.BlockSpec((None, ts, FOX_WIDTH), lambda bb, i: (bb, i, q_blk)))
        args.append(q_arr)
    in_specs += [
        pl.BlockSpec((None, ts, FOX_WIDTH), lambda bb, i: (bb, i, k_blk)),
        pl.BlockSpec((None, ts, FOX_WIDTH), lambda bb, i: (bb, i, v_blk)),
        pl.BlockSpec((None, ts, LANES), lambda bb, i: (bb, i, 0)),
        pl.BlockSpec((None, 1, LANES), lambda bb, i: (bb, 0, 0)),
    ]
    args += [k_arr, v_arr, lf_pad, init]
    out_specs, out_shape = [], []
    if has_q:
        out_specs.append(pl.BlockSpec((None, HEADS, ts, QK_WIDTH), head4))
        out_shape.append(jax.ShapeDtypeStruct((b, HEADS, s, QK_WIDTH), BF16))
    out_specs += [
        pl.BlockSpec((None, HEADS, ts, QK_WIDTH), head4),
        pl.BlockSpec((None, HEADS, 1, HEAD_DIM, ts), lambda bb, i: (bb, 0, i, 0, 0)),
        pl.BlockSpec((None, 1, LANES), lambda bb, i: (bb, 0, 0)),
    ]
    out_shape += [
        jax.ShapeDtypeStruct((b, HEADS, s, QK_WIDTH), BF16),
        jax.ShapeDtypeStruct((b, HEADS, s // ts, HEAD_DIM, ts), BF16),
        jax.ShapeDtypeStruct((b, 1, LANES), F32),
    ]
    return pl.pallas_call(
        functools.partial(_fox_prep_kernel, has_q=has_q),
        grid=(b, s // ts),
        in_specs=in_specs,
        out_specs=out_specs,
        out_shape=out_shape,
        scratch_shapes=[pltpu.VMEM((1, LANES), F32)],
        compiler_params=_params(2, V7X_VMEM_LIMIT),
        name="fox_prep_q" if has_q else "fox_prep_past",
    )(*args)


def _attn_kernel(*refs, tq, tkp, n_past, mask_shift, hp):
    if n_past:
        q_ref, kn_ref, vn_ref, kp_ref, vp_ref, o_ref = refs
    else:
        q_ref, kn_ref, vn_ref, o_ref = refs
    i = pl.program_id(2)

    def block(states, k_ref, off, tk, vt_ref, j, mask):
        scores = [lax.dot_general(k_ref[hh, pl.ds(off, tk), :], q_ref[hh],
                                  (((1,), (1,)), ((), ())), preferred_element_type=F32)
                  for hh in range(hp)]
        out = []
        for hh in range(hp):
            m_prev, l_prev, acc_prev = states[hh]
            s_t = scores[hh]
            if mask is not None:
                s_t = jnp.where(mask, s_t, NEG_INF)
            m_new = jnp.maximum(m_prev, jnp.max(s_t, axis=0, keepdims=True))
            alpha = jnp.exp2(m_prev - m_new)
            p_t = jnp.exp2(s_t - m_new)
            l_new = alpha * l_prev + jnp.sum(p_t, axis=0, keepdims=True)
            acc_new = alpha * acc_prev + jnp.dot(vt_ref[hh, j], p_t.astype(BF16),
                                                 preferred_element_type=F32)
            out.append((m_new, l_new, acc_new))
        return tuple(out)

    states = tuple((jnp.full((1, tq), NEG_INF, F32), jnp.zeros((1, tq), F32),
                    jnp.zeros((HEAD_DIM, tq), F32)) for _ in range(hp))

    if n_past:
        def past_body(j, st):
            return block(st, kp_ref, pl.multiple_of(j * tkp, tkp), tkp, vp_ref, j, None)
        states = lax.fori_loop(0, n_past, past_body, states)

    def new_body(j, st):
        return block(st, kn_ref, pl.multiple_of(j * tq, tq), tq, vn_ref, j, None)
    states = lax.fori_loop(0, i, new_body, states)

    key = lax.broadcasted_iota(jnp.int32, (tq, tq), 0)
    qry = lax.broadcasted_iota(jnp.int32, (tq, tq), 1)
    mask = (key >> mask_shift) <= (qry >> mask_shift)
    states = block(states, kn_ref, pl.multiple_of(i * tq, tq), tq, vn_ref, i, mask)
    for hh in range(hp):
        _, l_fin, acc_fin = states[hh]
        o_t = acc_fin / l_fin
        o_ref[:, hh * HEAD_DIM:(hh + 1) * HEAD_DIM] = _transpose_cols(o_t).astype(o_ref.dtype)


def _attention(q4, kn4, vn5, past, mask_shift):
    b, h, t, _ = q4.shape
    tq = _attn_tile(t)
    hp = ATTN_HEADS_PER_STEP
    assert h % hp == 0 and vn5.shape[4] == tq
    in_specs = [
        pl.BlockSpec((None, hp, tq, QK_WIDTH), lambda bb, g, i: (bb, g, i, 0)),
        pl.BlockSpec((None, hp, t, QK_WIDTH), lambda bb, g, i: (bb, g, 0, 0)),
        pl.BlockSpec((None, hp, t // tq, HEAD_DIM, tq), lambda bb, g, i: (bb, g, 0, 0, 0)),
    ]
    args = [q4, kn4, vn5]
    n_past, tkp = 0, 0
    if past is not None:
        kp4, vp5 = past
        p_len = kp4.shape[2]
        n_past, tkp = vp5.shape[2], vp5.shape[4]
        in_specs += [
            pl.BlockSpec((None, hp, p_len, QK_WIDTH), lambda bb, g, i: (bb, g, 0, 0)),
            pl.BlockSpec((None, hp, n_past, HEAD_DIM, tkp), lambda bb, g, i: (bb, g, 0, 0, 0)),
        ]
        args += [kp4, vp5]
    kern = functools.partial(_attn_kernel, tq=tq, tkp=tkp, n_past=n_past,
                             mask_shift=mask_shift, hp=hp)
    return pl.pallas_call(
        kern,
        grid=(b, h // hp, t // tq),
        in_specs=in_specs,
        out_specs=pl.BlockSpec((None, tq, hp * HEAD_DIM), lambda bb, g, i: (bb, i, g)),
        out_shape=jax.ShapeDtypeStruct((b, t, h * HEAD_DIM), BF16),
        compiler_params=_params(3, V7X_VMEM_LIMIT),
        name="attn_chunk" if mask_shift else "attn_frame",
    )(*args)


def _out_merge_kernel(oa_ref, ob_ref, ga_ref, gb_ref, x_ref, woa_ref, wob_ref, wo_ref,
                      h_ref, mg_ref, *, tc):
    d = x_ref.shape[1]
    oa = oa_ref[...]
    ob = ob_ref[...]
    for c in range(d // tc):
        sl = slice(c * tc, (c + 1) * tc)
        ta = jnp.dot(oa, woa_ref[:, sl], preferred_element_type=F32)
        tb = jnp.dot(ob, wob_ref[:, sl], preferred_element_type=F32)
        ga = jax.nn.sigmoid(ga_ref[:, sl].astype(F32))
        gb = jax.nn.sigmoid(gb_ref[:, sl].astype(F32))
        mg_ref[:, sl] = (ga * ta + gb * tb).astype(BF16)
    mg = mg_ref[...]
    for c in range(d // tc):
        sl = slice(c * tc, (c + 1) * tc)
        h_ref[:, sl] = x_ref[:, sl] + jnp.dot(mg, wo_ref[:, sl], preferred_element_type=F32)


def _out_merge(oa, ob, g_arr, x2d, woa, wob, wo):
    n, d = x2d.shape
    tm = _tile(n, 256)
    tc = _tile(d, 512)
    const = lambda i: (0, 0)
    one = pl.Buffered(1)
    return pl.pallas_call(
        functools.partial(_out_merge_kernel, tc=tc),
        grid=(n // tm,),
        in_specs=[
            pl.BlockSpec((tm, oa.shape[1]), lambda i: (i, 0)),
            pl.BlockSpec((tm, ob.shape[1]), lambda i: (i, 0)),
            pl.BlockSpec((tm, d), lambda i: (i, 0)),
            pl.BlockSpec((tm, d), lambda i: (i, 1)),
            pl.BlockSpec((tm, d), lambda i: (i, 0)),
            pl.BlockSpec(woa.shape, const, pipeline_mode=one),
            pl.BlockSpec(wob.shape, const, pipeline_mode=one),
            pl.BlockSpec(wo.shape, const, pipeline_mode=one),
        ],
        out_specs=pl.BlockSpec((tm, d), lambda i: (i, 0)),
        out_shape=jax.ShapeDtypeStruct((n, d), F32),
        scratch_shapes=[pltpu.VMEM((tm, d), BF16)],
        compiler_params=_params(1, V7X_VMEM_LIMIT),
        name="out_merge",
    )(oa, ob, g_arr, g_arr, x2d, woa, wob, wo)


def _norm_matmul_kernel(x_ref, g_ref, w_ref, o_ref, xn_ref):
    @pl.when(pl.program_id(1) == 0)
    def _():
        xn_ref[...] = _rms(x_ref[...], g_ref[...]).astype(BF16)

    o_ref[...] = jnp.dot(xn_ref[...], w_ref[...], preferred_element_type=F32).astype(o_ref.dtype)


def _norm_matmul(x2d, g, w, out_dtype):
    n, d = x2d.shape
    m = w.shape[1]
    tm = _tile(n, 1024)
    tn = _tile(m, 1024)
    assert tn % LANES == 0
    return pl.pallas_call(
        _norm_matmul_kernel,
        grid=(n // tm, m // tn),
        in_specs=[
            pl.BlockSpec((tm, d), lambda i, j: (i, 0)),
            pl.BlockSpec((1, d), lambda i, j: (0, 0)),
            pl.BlockSpec((d, tn), lambda i, j: (0, j)),
        ],
        out_specs=pl.BlockSpec((tm, tn), lambda i, j: (i, j)),
        out_shape=jax.ShapeDtypeStruct((n, m), out_dtype),
        scratch_shapes=[pltpu.VMEM((tm, d), BF16)],
        compiler_params=_params(2, V7X_VMEM_LIMIT),
        name="ffn_up",
    )(x2d, g, w)


def _ffn_down_kernel(uv_ref, ug_ref, hv_ref, hg_ref, wcv_ref, wcg_ref, bcv_ref, bcg_ref,
                     wd_ref, h_ref, o_ref, acc_ref):
    j = pl.program_id(1)

    @pl.when(j == 0)
    def _():
        acc_ref[...] = jnp.zeros(acc_ref.shape, F32)

    def conv(u_ref, halo_ref, wc_ref, bc_ref):
        u = u_ref[...].astype(F32)
        halo = halo_ref[...]
        row = lax.broadcasted_iota(jnp.int32, u.shape, 0)
        u1 = jnp.where(row == 0, halo[1:2, :], pltpu.roll(u, 1, axis=0))
        u2 = jnp.where(row == 0, halo[0:1, :],
                       jnp.where(row == 1, halo[1:2, :], pltpu.roll(u, 2, axis=0)))
        wc = wc_ref[...]
        return bc_ref[...] + u2 * wc[0:1, :] + u1 * wc[1:2, :] + u * wc[2:3, :]

    val = conv(uv_ref, hv_ref, wcv_ref, bcv_ref)
    gate = conv(ug_ref, hg_ref, wcg_ref, bcg_ref)
    cdf = 0.5 * (1.0 + jnp.tanh(0.7978845608028654 * (gate + 0.044715 * (gate * gate * gate))))
    act = (gate * cdf * val).astype(BF16)
    acc_ref[...] += jnp.dot(act, wd_ref[...], preferred_element_type=F32)

    @pl.when(j == pl.num_programs(1) - 1)
    def _():
        o_ref[...] = h_ref[...] + acc_ref[...]


def _ffn_down(u2d, halo, w_conv, b_conv, w_down, h2d, t):
    n, d = h2d.shape
    f = w_down.shape[0]
    tm = _tile(t, 512)
    tc = _tile(f, 512)
    nf = f // tc
    return pl.pallas_call(
        _ffn_down_kernel,
        grid=(n // tm, nf),
        in_specs=[
            pl.BlockSpec((tm, tc), lambda i, j: (i, j)),
            pl.BlockSpec((tm, tc), lambda i, j: (i, j + nf)),
            pl.BlockSpec((None, CONV_W - 1, tc), lambda i, j: (i, 0, j)),
            pl.BlockSpec((None, CONV_W - 1, tc), lambda i, j: (i, 0, j + nf)),
            pl.BlockSpec((CONV_W, tc), lambda i, j: (0, j)),
            pl.BlockSpec((CONV_W, tc), lambda i, j: (0, j + nf)),
            pl.BlockSpec((1, tc), lambda i, j: (0, j)),
            pl.BlockSpec((1, tc), lambda i, j: (0, j + nf)),
            pl.BlockSpec((tc, d), lambda i, j: (j, 0)),
            pl.BlockSpec((tm, d), lambda i, j: (i, 0)),
        ],
        out_specs=pl.BlockSpec((tm, d), lambda i, j: (i, 0)),
        out_shape=jax.ShapeDtypeStruct((n, d), F32),
        scratch_shapes=[pltpu.VMEM((tm, d), F32)],
        compiler_params=_params(2, V7X_VMEM_LIMIT),
        name="ffn_down",
    )(u2d, u2d, halo, halo, w_conv, w_conv, b_conv, b_conv, w_down, h2d)


def _ple_final_kernel(h_ref, pe_ref, gp_ref, gf_ref, wpg_ref, wple_ref, y_ref, h3_ref, *, tc):
    d = h_ref.shape[1]
    hn = _rms(h_ref[...], gp_ref[...]).astype(BF16)
    pe = pe_ref[...].astype(BF16)
    for c in range(d // tc):
        sl = slice(c * tc, (c + 1) * tc)
        gate = jax.nn.sigmoid(jnp.dot(hn, wpg_ref[:, sl], preferred_element_type=F32))
        val = jnp.dot(pe, wple_ref[:, sl], preferred_element_type=F32)
        h3_ref[:, sl] = h_ref[:, sl] + gate * val
    y_ref[...] = _rms(h3_ref[...], gf_ref[...])


def _ple_final(h2d, pe2d, g_ple, g_final, wpg, wple):
    n, d = h2d.shape
    tm = _tile(n, 256)
    tc = _tile(d, 512)
    const = lambda i: (0, 0)
    one = pl.Buffered(1)
    return pl.pallas_call(
        functools.partial(_ple_final_kernel, tc=tc),
        grid=(n // tm,),
        in_specs=[
            pl.BlockSpec((tm, d), lambda i: (i, 0)),
            pl.BlockSpec((tm, pe2d.shape[1]), lambda i: (i, 0)),
            pl.BlockSpec((1, d), const),
            pl.BlockSpec((1, d), const),
            pl.BlockSpec(wpg.shape, const, pipeline_mode=one),
            pl.BlockSpec(wple.shape, const, pipeline_mode=one),
        ],
        out_specs=pl.BlockSpec((tm, d), lambda i: (i, 0)),
        out_shape=jax.ShapeDtypeStruct((n, d), F32),
        scratch_shapes=[pltpu.VMEM((tm, d), F32)],
        compiler_params=_params(1, V7X_VMEM_LIMIT),
        name="ple_final",
    )(h2d, pe2d, g_ple, g_final, wpg, wple)


def _prep_weights(w, d_model):
    half = MLA_ROPE // 2
    o_cq = 0
    o_ckv = o_cq + MLA_Q_RANK
    o_kr = o_ckv + MLA_KV_RANK
    o_fq = o_kr + MLA_ROPE
    o_fk = o_fq + FOX_WIDTH
    o_fv = o_fk + FOX_WIDTH
    o_fl = o_fv + FOX_WIDTH
    o_ga = o_fl + FOX_HEADS
    o_gb = o_ga + d_model
    w_in = w["w_in"]
    d = w_in.shape[0]
    col = lambda o, n: w_in[:, o:o + n]
    w_all = jnp.concatenate([
        col(o_cq, MLA_Q_RANK), col(o_ckv, MLA_KV_RANK),
        col(o_fk, FOX_WIDTH), col(o_fv, FOX_WIDTH),
        col(o_ga, d_model), col(o_gb, d_model),
        col(o_fq, FOX_WIDTH) * (FOX_SCALE * LOG2E),
    ], axis=1).astype(BF16)
    zeros = lambda n: jnp.zeros((d, n), F32)
    w_s = jnp.concatenate([
        col(o_kr, MLA_ROPE), zeros(LANES - MLA_ROPE),
        col(o_kr + half, half), col(o_kr, half), zeros(LANES - MLA_ROPE),
        col(o_fl, FOX_HEADS), zeros(LANES - FOX_HEADS),
    ], axis=1).astype(BF16)

    qk = MLA_NOPE + MLA_ROPE
    wq = w["w_uq"].reshape(MLA_Q_RANK, HEADS, qk) * (MLA_SCALE * LOG2E)
    pad = jnp.zeros((MLA_Q_RANK, HEADS, LANES - MLA_ROPE), F32)
    nope = wq[:, :, :MLA_NOPE]
    x1 = wq[:, :, MLA_NOPE:MLA_NOPE + half]
    x2 = wq[:, :, MLA_NOPE + half:]
    rot = jnp.concatenate([x1, x2, pad], axis=2)
    rot_sw = jnp.concatenate([x2, x1, pad], axis=2)
    wuq3 = jnp.concatenate([nope.reshape(MLA_Q_RANK, -1), rot.reshape(MLA_Q_RANK, -1),
                            rot_sw.reshape(MLA_Q_RANK, -1)], axis=1).astype(BF16)
    wkv = w["w_ukv"].reshape(MLA_KV_RANK, HEADS, MLA_NOPE + MLA_V)
    wukv2 = jnp.concatenate([wkv[:, :, :MLA_NOPE].reshape(MLA_KV_RANK, -1),
                             wkv[:, :, MLA_NOPE:].reshape(MLA_KV_RANK, -1)], axis=1).astype(BF16)
    bf_pad = jnp.zeros((1, LANES), F32).at[0, :FOX_HEADS].set(w["b_f"])
    return dict(
        w_all=w_all, w_s=w_s, wuq3=wuq3, wukv2=wukv2, bf_pad=bf_pad,
        g_mix=w["g_mix"][None, :], g_q=w["g_q"][None, :], g_kv=w["g_kv"][None, :],
        w_oa=w["w_oa"].astype(BF16), w_ob=w["w_ob"].astype(BF16), w_o=w["w_o"].astype(BF16),
        g_ffn=w["g_ffn"][None, :], w_up=w["w_up"].astype(BF16),
        w_conv=w["w_conv"], b_conv=w["b_conv"][None, :], w_down=w["w_down"].astype(BF16),
        g_ple=w["g_ple"][None, :], w_pg=w["w_pg"].astype(BF16), w_ple=w["w_ple"].astype(BF16),
    )


def _rope_tables(pos):
    half = MLA_ROPE // 2
    inv = ROPE_THETA ** (-jnp.arange(half, dtype=F32) / half)
    ang = pos.astype(F32)[:, None] * inv[None, :]
    cos, sin = jnp.cos(ang), jnp.sin(ang)
    zero = jnp.zeros((pos.shape[0], LANES - MLA_ROPE), F32)
    return (jnp.concatenate([cos, cos, zero], axis=1),
            jnp.concatenate([-sin, sin, zero], axis=1))


def _layer(x, pe, past, pw, g_final):
    b, t, d = x.shape
    n = b * t
    p_len = 0 if past is None else past[0].shape[1]
    x2d = x.reshape(n, d)

    a2d, sm2d, kv2d, g2d = _in_proj(x2d, pw["g_mix"], pw["w_all"], pw["w_s"], d)
    cos_t, sin_t = _rope_tables(p_len + jnp.arange(t))
    (q_mla, ckv_n, kr_n, logf, logf_pad, kc_new, v_new) = _mla_pre(
        a2d.reshape(b, t, -1), sm2d.reshape(b, t, -1), cos_t, sin_t,
        pw["g_q"], pw["g_kv"], pw["bf_pad"], pw["wuq3"], pw["wukv2"])

    kv3 = kv2d.reshape(b, t, -1)
    g3 = g2d.reshape(b, t, -1)
    fq_blk = (2 * d) // FOX_WIDTH
    assert (2 * d) % FOX_WIDTH == 0
    if past is None:
        mla_past = None
        fox_past = None
        init = jnp.zeros((b, 1, LANES), F32)
    else:
        c_ckv, c_kr, c_fk, c_fv, c_logf, _ = past
        mla_past = _kv_up_call(c_ckv, c_kr, pw["wukv2"])
        lf_past = jnp.pad(c_logf, ((0, 0), (0, 0), (0, LANES - FOX_HEADS)))
        kp, vp, init = _fox_prep(None, (c_fk.reshape(b, p_len, -1), 0),
                                 (c_fv.reshape(b, p_len, -1), 0), lf_past,
                                 jnp.zeros((b, 1, LANES), F32))
        fox_past = (kp, vp)
    q_fox, k_fox, v_fox, _ = _fox_prep((g3, fq_blk), (kv3, 0), (kv3, 1), logf_pad, init)

    o_a = _attention(q_mla, kc_new, v_new, mla_past, int(math.log2(CHUNK)))
    o_b = _attention(q_fox, k_fox, v_fox, fox_past, 0)

    h2d = _out_merge(o_a.reshape(n, -1), o_b.reshape(n, -1), g2d, x2d,
                     pw["w_oa"], pw["w_ob"], pw["w_o"])

    f2 = pw["w_up"].shape[1]
    u2d = _norm_matmul(h2d, pw["g_ffn"], pw["w_up"], BF16)
    tm = _tile(t, 512)
    nt = t // tm
    prev = (jnp.zeros((b, CONV_W - 1, f2), F32) if past is None else past[5])
    tails = u2d.reshape(b, nt, tm, f2)[:, :nt - 1, tm - (CONV_W - 1):, :].astype(F32)
    halo = jnp.concatenate([prev[:, None], tails], axis=1).reshape(b * nt, CONV_W - 1, f2)
    h_last = h2d.reshape(b, t, d)[:, t - (CONV_W - 1):, :].reshape(b * (CONV_W - 1), d)
    new_conv = _norm_matmul(h_last, pw["g_ffn"], pw["w_up"], F32).reshape(b, CONV_W - 1, f2)
    h2d = _ffn_down(u2d, halo, pw["w_conv"], pw["b_conv"], pw["w_down"], h2d, t)

    y2d = _ple_final(h2d, pe.reshape(n, -1), pw["g_ple"], g_final[None, :], pw["w_pg"], pw["w_ple"])

    fk = kv3[:, :, :FOX_WIDTH].reshape(b, t, FOX_HEADS, FOX_HEAD_DIM)
    fv = kv3[:, :, FOX_WIDTH:].reshape(b, t, FOX_HEADS, FOX_HEAD_DIM)
    return y2d.reshape(b, t, d), (ckv_n, kr_n, fk, fv, logf, new_conv)


def kernel(x_prompt, x_sample, cache_mla_ckv, cache_mla_krope, cache_fox_k, cache_fox_v,
           cache_fox_logf, state_ffn_conv, p_prompt, p_sample, g_mix, w_in, b_f, g_q, w_uq,
           g_kv, w_ukv, w_oa, w_ob, w_o, g_ffn, w_up, w_conv, b_conv, w_down, g_ple, w_pg,
           w_ple, g_final):
    depth = w_in.shape[0]
    assert depth == 1, "the final norm is fused into the layer's last kernel"
    d_model = x_prompt.shape[-1]
    w = {"g_mix": g_mix[0], "w_in": w_in[0], "b_f": b_f[0], "g_q": g_q[0], "w_uq": w_uq[0],
         "g_kv": g_kv[0], "w_ukv": w_ukv[0], "w_oa": w_oa[0], "w_ob": w_ob[0], "w_o": w_o[0],
         "g_ffn": g_ffn[0], "w_up": w_up[0], "w_conv": w_conv[0], "b_conv": b_conv[0],
         "w_down": w_down[0], "g_ple": g_ple[0], "w_pg": w_pg[0], "w_ple": w_ple[0]}
    pw = _prep_weights(w, d_model)

    y_p, st_p = _layer(x_prompt, p_prompt[0], None, pw, g_final)
    past = (cache_mla_ckv[0], cache_mla_krope[0], cache_fox_k[0], cache_fox_v[0],
            cache_fox_logf[0], state_ffn_conv[0])
    y_s, st_s = _layer(x_sample, p_sample[0], past, pw, g_final)

    outs = [y_p, y_s]
    for j in range(6):
        outs.append(st_p[j][None])
        outs.append(st_s[j][None])
    return tuple(outs)
```

```python
import functools
import math

import jax
import jax.numpy as jnp
from jax import lax
from jax.experimental import pallas as pl
from jax.experimental.pallas import tpu as pltpu

F32 = jnp.float32
BF16 = jnp.bfloat16

CHUNK = 64
MLA_HEADS = 8
MLA_Q_RANK = 512
MLA_KV_RANK = 512
MLA_NOPE = 128
MLA_ROPE = 64
MLA_V = 128
MLA_SCALE = (MLA_NOPE + MLA_ROPE) ** -0.5
ROPE_THETA = 10000.0
FOX_HEADS = 8
FOX_HEAD_DIM = 128
FOX_WIDTH = FOX_HEADS * FOX_HEAD_DIM
FOX_SCALE = FOX_HEAD_DIM ** -0.5
CONV_W = 3
EPS = 1e-6
NEG_INF = -1e30
LOG2E = 1.4426950408889634

HEADS = 8
HEAD_DIM = 128
QK_WIDTH = 256
LANES = 128
ATTN_HEADS_PER_STEP = 4
V7X_VMEM_LIMIT = 56 * 1024 * 1024

assert MLA_HEADS == HEADS and FOX_HEADS == HEADS
assert MLA_NOPE == HEAD_DIM and MLA_V == HEAD_DIM and FOX_HEAD_DIM == HEAD_DIM


def _tile(n, pref):
    if n <= pref:
        return n
    for t in range(pref, 7, -1):
        if n % t == 0 and t % 8 == 0:
            return t
    return n


def _rms(x, g):
    ms = jnp.mean(x * x, axis=-1, keepdims=True)
    return x * lax.rsqrt(ms + EPS) * g


def _split3(x):
    a1 = x.astype(BF16).astype(F32)
    r1 = x - a1
    a2 = r1.astype(BF16).astype(F32)
    a3 = (r1 - a2).astype(BF16).astype(F32)
    return a1, a2, a3


def _attn_tile(t):
    return _tile(t, 512)


def _transpose_rows(x):
    r = x.shape[0]
    if r % LANES:
        x = jnp.concatenate([x, jnp.zeros((LANES - r % LANES, x.shape[1]), x.dtype)], axis=0)
    return x.T[:, :r]


def _transpose_cols(x):
    c = x.shape[1]
    if c % LANES:
        x = jnp.concatenate([x, jnp.zeros((x.shape[0], LANES - c % LANES), x.dtype)], axis=1)
    return x.T[:c, :]


def _params(n_axes, vmem=None):
    return pltpu.CompilerParams(
        dimension_semantics=("arbitrary",) * n_axes,
        vmem_limit_bytes=vmem,
    )


def _in_proj_kernel(x_ref, g_ref, w_ref, ws_ref, a_ref, s_ref, kv_ref, go_ref, xn_ref,
                    *, n_a, n_kv):
    j = pl.program_id(1)

    @pl.when(j == 0)
    def _():
        xn = _rms(x_ref[...], g_ref[...]).astype(BF16)
        xn_ref[...] = xn
        s_ref[...] = jnp.dot(xn, ws_ref[...], preferred_element_type=F32)

    def mm():
        return jnp.dot(xn_ref[...], w_ref[...], preferred_element_type=F32)

    @pl.when(j < n_a)
    def _():
        a_ref[...] = mm()

    @pl.when(jnp.logical_and(j >= n_a, j < n_a + n_kv))
    def _():
        kv_ref[...] = mm()

    @pl.when(j >= n_a + n_kv)
    def _():
        go_ref[...] = mm().astype(BF16)


def _in_proj(x2d, g, w_all, w_s, d_model):
    n, d = x2d.shape
    tm = _tile(n, 512)
    tn = math.gcd(1024, 2 * d_model)
    wa, wkv, wg = 2 * MLA_Q_RANK, 2 * FOX_WIDTH, 2 * d_model + FOX_WIDTH
    n_a, n_kv, n_g = wa // tn, wkv // tn, wg // tn
    ws = w_s.shape[1]
    kern = functools.partial(_in_proj_kernel, n_a=n_a, n_kv=n_kv)
    return pl.pallas_call(
        kern,
        grid=(n // tm, n_a + n_kv + n_g),
        in_specs=[
            pl.BlockSpec((tm, d), lambda i, j: (i, 0)),
            pl.BlockSpec((1, d), lambda i, j: (0, 0)),
            pl.BlockSpec((d, tn), lambda i, j: (0, j)),
            pl.BlockSpec((d, ws), lambda i, j: (0, 0)),
        ],
        out_specs=[
            pl.BlockSpec((tm, tn), lambda i, j: (i, jnp.minimum(j, n_a - 1))),
            pl.BlockSpec((tm, ws), lambda i, j: (i, 0)),
            pl.BlockSpec((tm, tn), lambda i, j: (i, jnp.clip(j - n_a, 0, n_kv - 1))),
            pl.BlockSpec((tm, tn), lambda i, j: (i, jnp.clip(j - n_a - n_kv, 0, n_g - 1))),
        ],
        out_shape=[
            jax.ShapeDtypeStruct((n, wa), F32),
            jax.ShapeDtypeStruct((n, ws), F32),
            jax.ShapeDtypeStruct((n, wkv), F32),
            jax.ShapeDtypeStruct((n, wg), BF16),
        ],
        scratch_shapes=[pltpu.VMEM((tm, d), BF16)],
        compiler_params=_params(2, V7X_VMEM_LIMIT),
        name="in_proj",
    )(x2d, g, w_all, w_s)


def _kv_up(ckvn, kr128, wukv_ref, kc_ref, v_ref):
    kv = jnp.dot(ckvn.astype(BF16), wukv_ref[...], preferred_element_type=F32)
    krb = kr128.astype(BF16)
    for h in range(HEADS):
        kc_ref[h, :, 0:HEAD_DIM] = kv[:, h * HEAD_DIM:(h + 1) * HEAD_DIM].astype(BF16)
        kc_ref[h, :, HEAD_DIM:QK_WIDTH] = krb
        v_ref[h, 0] = _transpose_rows(
            kv[:, (HEADS + h) * HEAD_DIM:(HEADS + h + 1) * HEAD_DIM]).astype(BF16)


def _mla_pre_kernel(a_ref, sm_ref, c_ref, s_ref, gq_ref, gkv_ref, bf_ref, wuq_ref, wukv_ref,
                    q_ref, ckv_ref, kr_ref, lf8_ref, lfp_ref, kc_ref, v_ref):
    a = a_ref[...]
    cos = c_ref[...]
    sin = s_ref[...]
    qn = _rms(a[:, :MLA_Q_RANK], gq_ref[...]).astype(BF16)
    q3 = jnp.dot(qn, wuq_ref[...], preferred_element_type=F32)
    hw = HEADS * HEAD_DIM
    for h in range(HEADS):
        lo, hi = h * HEAD_DIM, (h + 1) * HEAD_DIM
        q_ref[h, :, 0:HEAD_DIM] = q3[:, lo:hi].astype(BF16)
        rot = q3[:, hw + lo:hw + hi] * cos + q3[:, 2 * hw + lo:2 * hw + hi] * sin
        q_ref[h, :, HEAD_DIM:QK_WIDTH] = rot.astype(BF16)

    ckvn = _rms(a[:, MLA_Q_RANK:], gkv_ref[...])
    ckv_ref[...] = ckvn
    sm = sm_ref[...]
    kr128 = sm[:, 0:LANES] * cos + sm[:, LANES:2 * LANES] * sin
    kr_ref[...] = kr128[:, :MLA_ROPE]
    z = sm[:, 2 * LANES:3 * LANES] + bf_ref[...]
    lf = jnp.minimum(z, 0.0) - jnp.log1p(jnp.exp(-jnp.abs(z)))
    lane = lax.broadcasted_iota(jnp.int32, lf.shape, 1)
    lfp_ref[...] = jnp.where(lane < FOX_HEADS, lf, 0.0)
    lf8_ref[...] = lf[:, :FOX_HEADS]
    _kv_up(ckvn, kr128, wukv_ref, kc_ref, v_ref)


def _mla_pre(a3, sm3, cos_t, sin_t, gq, gkv, bf_pad, wuq3, wukv2):
    b, t, wa = a3.shape
    tm = _attn_tile(t)
    ws = sm3.shape[2]
    const = lambda bb, i: (0, 0)
    row3 = lambda bb, i: (bb, i, 0)
    head4 = lambda bb, i: (bb, 0, i, 0)
    vt5 = lambda bb, i: (bb, 0, i, 0, 0)
    return pl.pallas_call(
        _mla_pre_kernel,
        grid=(b, t // tm),
        in_specs=[
            pl.BlockSpec((None, tm, wa), row3),
            pl.BlockSpec((None, tm, ws), row3),
            pl.BlockSpec((tm, LANES), lambda bb, i: (i, 0)),
            pl.BlockSpec((tm, LANES), lambda bb, i: (i, 0)),
            pl.BlockSpec((1, MLA_Q_RANK), const),
            pl.BlockSpec((1, MLA_KV_RANK), const),
            pl.BlockSpec((1, LANES), const),
            pl.BlockSpec(wuq3.shape, const),
            pl.BlockSpec(wukv2.shape, const),
        ],
        out_specs=[
            pl.BlockSpec((None, HEADS, tm, QK_WIDTH), head4),
            pl.BlockSpec((None, tm, MLA_KV_RANK), row3),
            pl.BlockSpec((None, tm, MLA_ROPE), row3),
            pl.BlockSpec((None, tm, FOX_HEADS), row3),
            pl.BlockSpec((None, tm, LANES), row3),
            pl.BlockSpec((None, HEADS, tm, QK_WIDTH), head4),
            pl.BlockSpec((None, HEADS, 1, HEAD_DIM, tm), vt5),
        ],
        out_shape=[
            jax.ShapeDtypeStruct((b, HEADS, t, QK_WIDTH), BF16),
            jax.ShapeDtypeStruct((b, t, MLA_KV_RANK), F32),
            jax.ShapeDtypeStruct((b, t, MLA_ROPE), F32),
            jax.ShapeDtypeStruct((b, t, FOX_HEADS), F32),
            jax.ShapeDtypeStruct((b, t, LANES), F32),
            jax.ShapeDtypeStruct((b, HEADS, t, QK_WIDTH), BF16),
            jax.ShapeDtypeStruct((b, HEADS, t // tm, HEAD_DIM, tm), BF16),
        ],
        compiler_params=_params(2, V7X_VMEM_LIMIT),
        name="mla_pre",
    )(a3, sm3, cos_t, sin_t, gq, gkv, bf_pad, wuq3, wukv2)


def _kv_up_kernel(ckv_ref, kr_ref, wukv_ref, kc_ref, v_ref):
    kr = kr_ref[...]
    kr128 = jnp.concatenate([kr, jnp.zeros((kr.shape[0], LANES - MLA_ROPE), F32)], axis=1)
    _kv_up(ckv_ref[...], kr128, wukv_ref, kc_ref, v_ref)


def _kv_up_call(ckv3, kr3, wukv2):
    b, s, _ = ckv3.shape
    ts = _attn_tile(s)
    row3 = lambda bb, i: (bb, i, 0)
    head4 = lambda bb, i: (bb, 0, i, 0)
    return pl.pallas_call(
        _kv_up_kernel,
        grid=(b, s // ts),
        in_specs=[
            pl.BlockSpec((None, ts, MLA_KV_RANK), row3),
            pl.BlockSpec((None, ts, MLA_ROPE), row3),
            pl.BlockSpec(wukv2.shape, lambda bb, i: (0, 0)),
        ],
        out_specs=[
            pl.BlockSpec((None, HEADS, ts, QK_WIDTH), head4),
            pl.BlockSpec((None, HEADS, 1, HEAD_DIM, ts), lambda bb, i: (bb, 0, i, 0, 0)),
        ],
        out_shape=[
            jax.ShapeDtypeStruct((b, HEADS, s, QK_WIDTH), BF16),
            jax.ShapeDtypeStruct((b, HEADS, s // ts, HEAD_DIM, ts), BF16),
        ],
        compiler_params=_params(2, V7X_VMEM_LIMIT),
        name="kv_up",
    )(ckv3, kr3, wukv2)


def _fox_prep_kernel(*refs, has_q):
    if has_q:
        (q_ref, k_ref, v_ref, lf_ref, init_ref,
         qo_ref, ko_ref, vo_ref, last_ref, carry_ref) = refs
    else:
        (k_ref, v_ref, lf_ref, init_ref, ko_ref, vo_ref, last_ref, carry_ref) = refs
    i = pl.program_id(1)

    @pl.when(i == 0)
    def _():
        carry_ref[...] = init_ref[...]

    lf = lf_ref[...]
    ts = lf.shape[0]
    row = lax.broadcasted_iota(jnp.int32, (ts, ts), 0)
    col = lax.broadcasted_iota(jnp.int32, (ts, ts), 1)
    tri = jnp.where(col <= row, 1.0, 0.0).astype(BF16)
    cum = carry_ref[...]
    for part in _split3(lf):
        cum = cum + jnp.dot(tri, part.astype(BF16), preferred_element_type=F32)
    carry_ref[...] = cum[ts - 1:ts, :]
    last_ref[...] = cum[ts - 1:ts, :]

    c1, c2, c3 = _split3(cum * LOG2E)
    lane = lax.broadcasted_iota(jnp.int32, (ts, LANES), 1)
    k = k_ref[...]
    v = v_ref[...]
    for h in range(HEADS):
        lo, hi = h * HEAD_DIM, (h + 1) * HEAD_DIM
        h1, h2, h3 = c1[:, h:h + 1], c2[:, h:h + 1], c3[:, h:h + 1]
        ek = jnp.where(lane < 3, 1.0,
                       jnp.where(lane == 3, -h1,
                                 jnp.where(lane == 4, -h2,
                                           jnp.where(lane == 5, -h3, 0.0))))
        ko_ref[h, :, 0:HEAD_DIM] = k[:, lo:hi].astype(BF16)
        ko_ref[h, :, HEAD_DIM:QK_WIDTH] = ek.astype(BF16)
        vo_ref[h, 0] = _transpose_rows(v[:, lo:hi]).astype(BF16)
        if has_q:
            eq = jnp.where(lane == 0, h1,
                           jnp.where(lane == 1, h2,
                                     jnp.where(lane == 2, h3,
                                               jnp.where(lane < 6, 1.0, 0.0))))
            qo_ref[h, :, 0:HEAD_DIM] = q_ref[:, lo:hi]
            qo_ref[h, :, HEAD_DIM:QK_WIDTH] = eq.astype(BF16)


def _fox_prep(q_src, k_src, v_src, lf_pad, init):
    k_arr, k_blk = k_src
    v_arr, v_blk = v_src
    b, s, _ = k_arr.shape
    ts = _attn_tile(s)
    has_q = q_src is not None
    head4 = lambda bb, i: (bb, 0, i, 0)
    in_specs, args = [], []
    if has_q:
        q_arr, q_blk = q_src
        in_specs.append(pl.BlockSpec((None, ts, FOX_WIDTH), lambda bb, i: (bb, i, q_blk)))
        args.append(q_arr)
    in_specs += [
        pl.BlockSpec((None, ts, FOX_WIDTH), lambda bb, i: (bb, i, k_blk)),
        pl.BlockSpec((None, ts, FOX_WIDTH), lambda bb, i: (bb, i, v_blk)),
        pl.BlockSpec((None, ts, LANES), lambda bb, i: (bb, i, 0)),
        pl.BlockSpec((None, 1, LANES), lambda bb, i: (bb, 0, 0)),
    ]
    args += [k_arr, v_arr, lf_pad, init]
    out_specs, out_shape = [], []
    if has_q:
        out_specs.append(pl.BlockSpec((None, HEADS, ts, QK_WIDTH), head4))
        out_shape.append(jax.ShapeDtypeStruct((b, HEADS, s, QK_WIDTH), BF16))
    out_specs += [
        pl.BlockSpec((None, HEADS, ts, QK_WIDTH), head4),
        pl.BlockSpec((None, HEADS, 1, HEAD_DIM, ts), lambda bb, i: (bb, 0, i, 0, 0)),
        pl.BlockSpec((None, 1, LANES), lambda bb, i: (bb, 0, 0)),
    ]
    out_shape += [
        jax.ShapeDtypeStruct((b, HEADS, s, QK_WIDTH), BF16),
        jax.ShapeDtypeStruct((b, HEADS, s // ts, HEAD_DIM, ts), BF16),
        jax.ShapeDtypeStruct((b, 1, LANES), F32),
    ]
    return pl.pallas_call(
        functools.partial(_fox_prep_kernel, has_q=has_q),
        grid=(b, s // ts),
        in_specs=in_specs,
        out_specs=out_specs,
        out_shape=out_shape,
        scratch_shapes=[pltpu.VMEM((1, LANES), F32)],
        compiler_params=_params(2, V7X_VMEM_LIMIT),
        name="fox_prep_q" if has_q else "fox_prep_past",
    )(*args)


def _attn_kernel(*refs, tq, tkp, n_past, mask_shift, hp):
    if n_past:
        q_ref, kn_ref, vn_ref, kp_ref, vp_ref, o_ref = refs
    else:
        q_ref, kn_ref, vn_ref, o_ref = refs
    i = pl.program_id(2)

    def block(states, k_ref, off, tk, vt_ref, j, mask):
        scores = [lax.dot_general(k_ref[hh, pl.ds(off, tk), :], q_ref[hh],
                                  (((1,), (1,)), ((), ())), preferred_element_type=F32)
                  for hh in range(hp)]
        out = []
        for hh in range(hp):
            m_prev, l_prev, acc_prev = states[hh]
            s_t = scores[hh]
            if mask is not None:
                s_t = jnp.where(mask, s_t, NEG_INF)
            m_new = jnp.maximum(m_prev, jnp.max(s_t, axis=0, keepdims=True))
            alpha = jnp.exp2(m_prev - m_new)
            p_t = jnp.exp2(s_t - m_new)
            l_new = alpha * l_prev + jnp.sum(p_t, axis=0, keepdims=True)
            acc_new = alpha * acc_prev + jnp.dot(vt_ref[hh, j], p_t.astype(BF16),
                                                 preferred_element_type=F32)
            out.append((m_new, l_new, acc_new))
        return tuple(out)

    states = tuple((jnp.full((1, tq), NEG_INF, F32), jnp.zeros((1, tq), F32),
                    jnp.zeros((HEAD_DIM, tq), F32)) for _ in range(hp))

    if n_past:
        def past_body(j, st):
            return block(st, kp_ref, pl.multiple_of(j * tkp, tkp), tkp, vp_ref, j, None)
        states = lax.fori_loop(0, n_past, past_body, states)

    def new_body(j, st):
        return block(st, kn_ref, pl.multiple_of(j * tq, tq), tq, vn_ref, j, None)
    states = lax.fori_loop(0, i, new_body, states)

    key = lax.broadcasted_iota(jnp.int32, (tq, tq), 0)
    qry = lax.broadcasted_iota(jnp.int32, (tq, tq), 1)
    mask = (key >> mask_shift) <= (qry >> mask_shift)
    states = block(states, kn_ref, pl.multiple_of(i * tq, tq), tq, vn_ref, i, mask)
    for hh in range(hp):
        _, l_fin, acc_fin = states[hh]
        o_t = acc_fin / l_fin
        o_ref[:, hh * HEAD_DIM:(hh + 1) * HEAD_DIM] = _transpose_cols(o_t).astype(o_ref.dtype)


def _attention(q4, kn4, vn5, past, mask_shift):
    b, h, t, _ = q4.shape
    tq = _attn_tile(t)
    hp = ATTN_HEADS_PER_STEP
    assert h % hp == 0 and vn5.shape[4] == tq
    in_specs = [
        pl.BlockSpec((None, hp, tq, QK_WIDTH), lambda bb, g, i: (bb, g, i, 0)),
        pl.BlockSpec((None, hp, t, QK_WIDTH), lambda bb, g, i: (bb, g, 0, 0)),
        pl.BlockSpec((None, hp, t // tq, HEAD_DIM, tq), lambda bb, g, i: (bb, g, 0, 0, 0)),
    ]
    args = [q4, kn4, vn5]
    n_past, tkp = 0, 0
    if past is not None:
        kp4, vp5 = past
        p_len = kp4.shape[2]
        n_past, tkp = vp5.shape[2], vp5.shape[4]
        in_specs += [
            pl.BlockSpec((None, hp, p_len, QK_WIDTH), lambda bb, g, i: (bb, g, 0, 0)),
            pl.BlockSpec((None, hp, n_past, HEAD_DIM, tkp), lambda bb, g, i: (bb, g, 0, 0, 0)),
        ]
        args += [kp4, vp5]
    kern = functools.partial(_attn_kernel, tq=tq, tkp=tkp, n_past=n_past,
                             mask_shift=mask_shift, hp=hp)
    return pl.pallas_call(
        kern,
        grid=(b, h // hp, t // tq),
        in_specs=in_specs,
        out_specs=pl.BlockSpec((None, tq, hp * HEAD_DIM), lambda bb, g, i: (bb, i, g)),
        out_shape=jax.ShapeDtypeStruct((b, t, h * HEAD_DIM), BF16),
        compiler_params=_params(3, V7X_VMEM_LIMIT),
        name="attn_chunk" if mask_shift else "attn_frame",
    )(*args)


def _out_merge_kernel(oa_ref, ob_ref, ga_ref, gb_ref, x_ref, woa_ref, wob_ref, wo_ref,
                      h_ref, mg_ref, *, tc):
    d = x_ref.shape[1]
    oa = oa_ref[...]
    ob = ob_ref[...]
    for c in range(d // tc):
        sl = slice(c * tc, (c + 1) * tc)
        ta = jnp.dot(oa, woa_ref[:, sl], preferred_element_type=F32)
        tb = jnp.dot(ob, wob_ref[:, sl], preferred_element_type=F32)
        ga = jax.nn.sigmoid(ga_ref[:, sl].astype(F32))
        gb = jax.nn.sigmoid(gb_ref[:, sl].astype(F32))
        mg_ref[:, sl] = (ga * ta + gb * tb).astype(BF16)
    mg = mg_ref[...]
    for c in range(d // tc):
        sl = slice(c * tc, (c + 1) * tc)
        h_ref[:, sl] = x_ref[:, sl] + jnp.dot(mg, wo_ref[:, sl], preferred_element_type=F32)


def _out_merge(oa, ob, g_arr, x2d, woa, wob, wo):
    n, d = x2d.shape
    tm = _tile(n, 256)
    tc = _tile(d, 512)
    const = lambda i: (0, 0)
    one = pl.Buffered(1)
    return pl.pallas_call(
        functools.partial(_out_merge_kernel, tc=tc),
        grid=(n // tm,),
        in_specs=[
            pl.BlockSpec((tm, oa.shape[1]), lambda i: (i, 0)),
            pl.BlockSpec((tm, ob.shape[1]), lambda i: (i, 0)),
            pl.BlockSpec((tm, d), lambda i: (i, 0)),
            pl.BlockSpec((tm, d), lambda i: (i, 1)),
            pl.BlockSpec((tm, d), lambda i: (i, 0)),
            pl.BlockSpec(woa.shape, const, pipeline_mode=one),
            pl.BlockSpec(wob.shape, const, pipeline_mode=one),
            pl.BlockSpec(wo.shape, const, pipeline_mode=one),
        ],
        out_specs=pl.BlockSpec((tm, d), lambda i: (i, 0)),
        out_shape=jax.ShapeDtypeStruct((n, d), F32),
        scratch_shapes=[pltpu.VMEM((tm, d), BF16)],
        compiler_params=_params(1, V7X_VMEM_LIMIT),
        name="out_merge",
    )(oa, ob, g_arr, g_arr, x2d, woa, wob, wo)


def _ffn_kernel(h_ref, g_ref, wuv_ref, wug_ref, pv_ref, pg_ref, wcv_ref, wcg_ref,
                bcv_ref, bcg_ref, wd_ref, o_ref, ncv_ref, ncg_ref,
                hn_ref, acc_ref, act0_ref, act1_ref, cv_ref, cg_ref,
                *, seg, tiles_per_seq, nf):
    act_refs = (act0_ref, act1_ref)
    i = pl.program_id(0)
    j = pl.program_id(1)
    tm = h_ref.shape[0]
    nseg = tm // seg
    keep = CONV_W - 1
    seq_start = (i % tiles_per_seq) == 0

    def conv(w_ref, prev_ref, carry_ref, wc_ref, bc_ref, nc_ref):
        u = jnp.dot(hn_ref[...], w_ref[...], preferred_element_type=F32)
        wc = wc_ref[...]
        bc = bc_ref[...]
        outs = []
        for s in range(nseg):
            us = u[s * seg:(s + 1) * seg, :]
            tail = us[seg - keep:seg, :]
            if nseg == 1:
                halo = jnp.where(seq_start, prev_ref[0], carry_ref[j])
                carry_ref[j] = tail
            else:
                halo = prev_ref[s]
            nc_ref[s] = tail
            row = lax.broadcasted_iota(jnp.int32, us.shape, 0)
            u1 = jnp.where(row == 0, halo[1:2, :], pltpu.roll(us, 1, axis=0))
            u2 = jnp.where(row == 0, halo[0:1, :],
                           jnp.where(row == 1, halo[1:2, :], pltpu.roll(us, 2, axis=0)))
            outs.append(bc + u2 * wc[0:1, :] + u1 * wc[1:2, :] + us * wc[2:3, :])
        return outs

    def up(slot):
        vals = conv(wuv_ref, pv_ref, cv_ref, wcv_ref, bcv_ref, ncv_ref)
        gates = conv(wug_ref, pg_ref, cg_ref, wcg_ref, bcg_ref, ncg_ref)
        for s in range(nseg):
            gate = gates[s]
            cdf = 0.5 * (1.0 + jnp.tanh(
                0.7978845608028654 * (gate + 0.044715 * (gate * gate * gate))))
            act_refs[slot][s * seg:(s + 1) * seg, :] = (gate * cdf * vals[s]).astype(BF16)

    def down(slot):
        acc_ref[...] += jnp.dot(act_refs[slot][...], wd_ref[...], preferred_element_type=F32)

    @pl.when(j == 0)
    def _():
        hn_ref[...] = _rms(h_ref[...], g_ref[...]).astype(BF16)
        acc_ref[...] = jnp.zeros(acc_ref.shape, F32)
        up(0)

    for parity in range(2):
        @pl.when(jnp.logical_and(jnp.logical_and(j >= 1, j < nf), j % 2 == parity))
        def _():
            up(parity)
            down(1 - parity)

    @pl.when(j == nf)
    def _():
        down((nf - 1) % 2)
        o_ref[...] = h_ref[...] + acc_ref[...]


def _ffn(h2d, prev, g, w_up, w_conv, b_conv, w_down, t):
    n, d = h2d.shape
    f = w_down.shape[0]
    b = n // t
    keep = CONV_W - 1
    assert CONV_W == 3
    tm = _tile(n, 512)
    if t >= tm:
        tm = _tile(t, 512)
        seg, nb_t, tiles_per_seq = tm, 1, t // tm
        bidx = lambda i: i // tiles_per_seq
    else:
        assert tm % t == 0
        seg, nb_t, tiles_per_seq = t, tm // t, 1
        bidx = lambda i: i
    tc = _tile(f, 512)
    nf = f // tc
    up_j = lambda j: jnp.minimum(j, nf - 1)
    down_j = lambda j: jnp.maximum(j - 1, 0)
    h2, ncv, ncg = pl.pallas_call(
        functools.partial(_ffn_kernel, seg=seg, tiles_per_seq=tiles_per_seq, nf=nf),
        grid=(n // tm, nf + 1),
        in_specs=[
            pl.BlockSpec((tm, d), lambda i, j: (i, 0)),
            pl.BlockSpec((1, d), lambda i, j: (0, 0)),
            pl.BlockSpec((d, tc), lambda i, j: (0, up_j(j))),
            pl.BlockSpec((d, tc), lambda i, j: (0, up_j(j) + nf)),
            pl.BlockSpec((nb_t, keep, tc), lambda i, j: (bidx(i), 0, up_j(j))),
            pl.BlockSpec((nb_t, keep, tc), lambda i, j: (bidx(i), 0, up_j(j) + nf)),
            pl.BlockSpec((CONV_W, tc), lambda i, j: (0, up_j(j))),
            pl.BlockSpec((CONV_W, tc), lambda i, j: (0, up_j(j) + nf)),
            pl.BlockSpec((1, tc), lambda i, j: (0, up_j(j))),
            pl.BlockSpec((1, tc), lambda i, j: (0, up_j(j) + nf)),
            pl.BlockSpec((tc, d), lambda i, j: (down_j(j), 0)),
        ],
        out_specs=[
            pl.BlockSpec((tm, d), lambda i, j: (i, 0)),
            pl.BlockSpec((nb_t, keep, tc), lambda i, j: (i, 0, up_j(j))),
            pl.BlockSpec((nb_t, keep, tc), lambda i, j: (i, 0, up_j(j))),
        ],
        out_shape=[
            jax.ShapeDtypeStruct((n, d), F32),
            jax.ShapeDtypeStruct((b * tiles_per_seq, keep, f), F32),
            jax.ShapeDtypeStruct((b * tiles_per_seq, keep, f), F32),
        ],
        scratch_shapes=[
            pltpu.VMEM((tm, d), BF16),
            pltpu.VMEM((tm, d), F32),
            pltpu.VMEM((tm, tc), BF16),
            pltpu.VMEM((tm, tc), BF16),
            pltpu.VMEM((nf, keep, tc), F32),
            pltpu.VMEM((nf, keep, tc), F32),
        ],
        compiler_params=_params(2, V7X_VMEM_LIMIT),
        name="ffn",
    )(h2d, g, w_up, w_up, prev, prev, w_conv, w_conv, b_conv, b_conv, w_down)
    last = lambda a: a.reshape(b, tiles_per_seq, keep, f)[:, tiles_per_seq - 1]
    return h2, jnp.concatenate([last(ncv), last(ncg)], axis=-1)


def _ple_final_kernel(h_ref, pe_ref, gp_ref, gf_ref, wpg_ref, wple_ref, y_ref, h3_ref, *, tc):
    d = h_ref.shape[1]
    hn = _rms(h_ref[...], gp_ref[...]).astype(BF16)
    pe = pe_ref[...].astype(BF16)
    for c in range(d // tc):
        sl = slice(c * tc, (c + 1) * tc)
        gate = jax.nn.sigmoid(jnp.dot(hn, wpg_ref[:, sl], preferred_element_type=F32))
        val = jnp.dot(pe, wple_ref[:, sl], preferred_element_type=F32)
        h3_ref[:, sl] = h_ref[:, sl] + gate * val
    y_ref[...] = _rms(h3_ref[...], gf_ref[...])


def _ple_final(h2d, pe2d, g_ple, g_final, wpg, wple):
    n, d = h2d.shape
    tm = _tile(n, 256)
    tc = _tile(d, 512)
    const = lambda i: (0, 0)
    one = pl.Buffered(1)
    return pl.pallas_call(
        functools.partial(_ple_final_kernel, tc=tc),
        grid=(n // tm,),
        in_specs=[
            pl.BlockSpec((tm, d), lambda i: (i, 0)),
            pl.BlockSpec((tm, pe2d.shape[1]), lambda i: (i, 0)),
            pl.BlockSpec((1, d), const),
            pl.BlockSpec((1, d), const),
            pl.BlockSpec(wpg.shape, const, pipeline_mode=one),
            pl.BlockSpec(wple.shape, const, pipeline_mode=one),
        ],
        out_specs=pl.BlockSpec((tm, d), lambda i: (i, 0)),
        out_shape=jax.ShapeDtypeStruct((n, d), F32),
        scratch_shapes=[pltpu.VMEM((tm, d), F32)],
        compiler_params=_params(1, V7X_VMEM_LIMIT),
        name="ple_final",
    )(h2d, pe2d, g_ple, g_final, wpg, wple)


def _prep_weights(w, d_model):
    half = MLA_ROPE // 2
    o_cq = 0
    o_ckv = o_cq + MLA_Q_RANK
    o_kr = o_ckv + MLA_KV_RANK
    o_fq = o_kr + MLA_ROPE
    o_fk = o_fq + FOX_WIDTH
    o_fv = o_fk + FOX_WIDTH
    o_fl = o_fv + FOX_WIDTH
    o_ga = o_fl + FOX_HEADS
    o_gb = o_ga + d_model
    w_in = w["w_in"]
    d = w_in.shape[0]
    col = lambda o, n: w_in[:, o:o + n]
    w_all = jnp.concatenate([
        col(o_cq, MLA_Q_RANK), col(o_ckv, MLA_KV_RANK),
        col(o_fk, FOX_WIDTH), col(o_fv, FOX_WIDTH),
        col(o_ga, d_model), col(o_gb, d_model),
        col(o_fq, FOX_WIDTH) * (FOX_SCALE * LOG2E),
    ], axis=1).astype(BF16)
    zeros = lambda n: jnp.zeros((d, n), F32)
    w_s = jnp.concatenate([
        col(o_kr, MLA_ROPE), zeros(LANES - MLA_ROPE),
        col(o_kr + half, half), col(o_kr, half), zeros(LANES - MLA_ROPE),
        col(o_fl, FOX_HEADS), zeros(LANES - FOX_HEADS),
    ], axis=1).astype(BF16)

    qk = MLA_NOPE + MLA_ROPE
    wq = w["w_uq"].reshape(MLA_Q_RANK, HEADS, qk) * (MLA_SCALE * LOG2E)
    pad = jnp.zeros((MLA_Q_RANK, HEADS, LANES - MLA_ROPE), F32)
    nope = wq[:, :, :MLA_NOPE]
    x1 = wq[:, :, MLA_NOPE:MLA_NOPE + half]
    x2 = wq[:, :, MLA_NOPE + half:]
    rot = jnp.concatenate([x1, x2, pad], axis=2)
    rot_sw = jnp.concatenate([x2, x1, pad], axis=2)
    wuq3 = jnp.concatenate([nope.reshape(MLA_Q_RANK, -1), rot.reshape(MLA_Q_RANK, -1),
                            rot_sw.reshape(MLA_Q_RANK, -1)], axis=1).astype(BF16)
    wkv = w["w_ukv"].reshape(MLA_KV_RANK, HEADS, MLA_NOPE + MLA_V)
    wukv2 = jnp.concatenate([wkv[:, :, :MLA_NOPE].reshape(MLA_KV_RANK, -1),
                             wkv[:, :, MLA_NOPE:].reshape(MLA_KV_RANK, -1)], axis=1).astype(BF16)
    bf_pad = jnp.zeros((1, LANES), F32).at[0, :FOX_HEADS].set(w["b_f"])
    return dict(
        w_all=w_all, w_s=w_s, wuq3=wuq3, wukv2=wukv2, bf_pad=bf_pad,
        g_mix=w["g_mix"][None, :], g_q=w["g_q"][None, :], g_kv=w["g_kv"][None, :],
        w_oa=w["w_oa"].astype(BF16), w_ob=w["w_ob"].astype(BF16), w_o=w["w_o"].astype(BF16),
        g_ffn=w["g_ffn"][None, :], w_up=w["w_up"].astype(BF16),
        w_conv=w["w_conv"], b_conv=w["b_conv"][None, :], w_down=w["w_down"].astype(BF16),
        g_ple=w["g_ple"][None, :], w_pg=w["w_pg"].astype(BF16), w_ple=w["w_ple"].astype(BF16),
    )


def _rope_tables(pos):
    half = MLA_ROPE // 2
    inv = ROPE_THETA ** (-jnp.arange(half, dtype=F32) / half)
    ang = pos.astype(F32)[:, None] * inv[None, :]
    cos, sin = jnp.cos(ang), jnp.sin(ang)
    zero = jnp.zeros((pos.shape[0], LANES - MLA_ROPE), F32)
    return (jnp.concatenate([cos, cos, zero], axis=1),
            jnp.concatenate([-sin, sin, zero], axis=1))


def _layer(x, pe, past, pw, g_final):
    b, t, d = x.shape
    n = b * t
    p_len = 0 if past is None else past[0].shape[1]
    x2d = x.reshape(n, d)

    a2d, sm2d, kv2d, g2d = _in_proj(x2d, pw["g_mix"], pw["w_all"], pw["w_s"], d)
    cos_t, sin_t = _rope_tables(p_len + jnp.arange(t))
    (q_mla, ckv_n, kr_n, logf, logf_pad, kc_new, v_new) = _mla_pre(
        a2d.reshape(b, t, -1), sm2d.reshape(b, t, -1), cos_t, sin_t,
        pw["g_q"], pw["g_kv"], pw["bf_pad"], pw["wuq3"], pw["wukv2"])

    kv3 = kv2d.reshape(b, t, -1)
    g3 = g2d.reshape(b, t, -1)
    fq_blk = (2 * d) // FOX_WIDTH
    assert (2 * d) % FOX_WIDTH == 0
    if past is None:
        mla_past = None
        fox_past = None
        init = jnp.zeros((b, 1, LANES), F32)
    else:
        c_ckv, c_kr, c_fk, c_fv, c_logf, _ = past
        mla_past = _kv_up_call(c_ckv, c_kr, pw["wukv2"])
        lf_past = jnp.pad(c_logf, ((0, 0), (0, 0), (0, LANES - FOX_HEADS)))
        kp, vp, init = _fox_prep(None, (c_fk.reshape(b, p_len, -1), 0),
                                 (c_fv.reshape(b, p_len, -1), 0), lf_past,
                                 jnp.zeros((b, 1, LANES), F32))
        fox_past = (kp, vp)
    q_fox, k_fox, v_fox, _ = _fox_prep((g3, fq_blk), (kv3, 0), (kv3, 1), logf_pad, init)

    o_a = _attention(q_mla, kc_new, v_new, mla_past, int(math.log2(CHUNK)))
    o_b = _attention(q_fox, k_fox, v_fox, fox_past, 0)

    h2d = _out_merge(o_a.reshape(n, -1), o_b.reshape(n, -1), g2d, x2d,
                     pw["w_oa"], pw["w_ob"], pw["w_o"])

    f2 = pw["w_up"].shape[1]
    prev = (jnp.zeros((b, CONV_W - 1, f2), F32) if past is None else past[5])
    h2d, new_conv = _ffn(h2d, prev, pw["g_ffn"], pw["w_up"], pw["w_conv"], pw["b_conv"],
                         pw["w_down"], t)

    y2d = _ple_final(h2d, pe.reshape(n, -1), pw["g_ple"], g_final[None, :], pw["w_pg"], pw["w_ple"])

    fk = kv3[:, :, :FOX_WIDTH].reshape(b, t, FOX_HEADS, FOX_HEAD_DIM)
    fv = kv3[:, :, FOX_WIDTH:].reshape(b, t, FOX_HEADS, FOX_HEAD_DIM)
    return y2d.reshape(b, t, d), (ckv_n, kr_n, fk, fv, logf, new_conv)


def kernel(x_prompt, x_sample, cache_mla_ckv, cache_mla_krope, cache_fox_k, cache_fox_v,
           cache_fox_logf, state_ffn_conv, p_prompt, p_sample, g_mix, w_in, b_f, g_q, w_uq,
           g_kv, w_ukv, w_oa, w_ob, w_o, g_ffn, w_up, w_conv, b_conv, w_down, g_ple, w_pg,
           w_ple, g_final):
    depth = w_in.shape[0]
    assert depth == 1, "the final norm is fused into the layer's last kernel"
    d_model = x_prompt.shape[-1]
    w = {"g_mix": g_mix[0], "w_in": w_in[0], "b_f": b_f[0], "g_q": g_q[0], "w_uq": w_uq[0],
         "g_kv": g_kv[0], "w_ukv": w_ukv[0], "w_oa": w_oa[0], "w_ob": w_ob[0], "w_o": w_o[0],
         "g_ffn": g_ffn[0], "w_up": w_up[0], "w_conv": w_conv[0], "b_conv": b_conv[0],
         "w_down": w_down[0], "g_ple": g_ple[0], "w_pg": w_pg[0], "w_ple": w_ple[0]}
    pw = _prep_weights(w, d_model)

    y_p, st_p = _layer(x_prompt, p_prompt[0], None, pw, g_final)
    past = (cache_mla_ckv[0], cache_mla_krope[0], cache_fox_k[0], cache_fox_v[0],
            cache_fox_logf[0], state_ffn_conv[0])
    y_s, st_s = _layer(x_sample, p_sample[0], past, pw, g_final)

    outs = [y_p, y_s]
    for j in range(6):
        outs.append(st_p[j][None])
        outs.append(st_s[j][None])
    return tuple(outs)
```

```python
import functools
import math

import jax
import jax.numpy as jnp
from jax import lax
from jax.experimental import pallas as pl
from jax.experimental.pallas import tpu as pltpu

F32 = jnp.float32
BF16 = jnp.bfloat16

CHUNK = 64
MLA_HEADS = 8
MLA_Q_RANK = 512
MLA_KV_RANK = 512
MLA_NOPE = 128
MLA_ROPE = 64
MLA_V = 128
MLA_SCALE = (MLA_NOPE + MLA_ROPE) ** -0.5
ROPE_THETA = 10000.0
FOX_HEADS = 8
FOX_HEAD_DIM = 128
FOX_WIDTH = FOX_HEADS * FOX_HEAD_DIM
FOX_SCALE = FOX_HEAD_DIM ** -0.5
CONV_W = 3
EPS = 1e-6
NEG_INF = -1e30
LOG2E = 1.4426950408889634

HEADS = 8
HEAD_DIM = 128
QK_WIDTH = 256
LANES = 128
ATTN_HEADS_PER_STEP = 4
V7X_VMEM_LIMIT = 56 * 1024 * 1024

assert MLA_HEADS == HEADS and FOX_HEADS == HEADS
assert MLA_NOPE == HEAD_DIM and MLA_V == HEAD_DIM and FOX_HEAD_DIM == HEAD_DIM


def _tile(n, pref):
    if n <= pref:
        return n
    for t in range(pref, 7, -1):
        if n % t == 0 and t % 8 == 0:
            return t
    return n


def _rms(x, g):
    ms = jnp.mean(x * x, axis=-1, keepdims=True)
    return x * lax.rsqrt(ms + EPS) * g


def _split3(x):
    a1 = x.astype(BF16).astype(F32)
    r1 = x - a1
    a2 = r1.astype(BF16).astype(F32)
    a3 = (r1 - a2).astype(BF16).astype(F32)
    return a1, a2, a3


def _attn_tile(t):
    return _tile(t, 512)


def _transpose_rows(x):
    r = x.shape[0]
    if r % LANES:
        x = jnp.concatenate([x, jnp.zeros((LANES - r % LANES, x.shape[1]), x.dtype)], axis=0)
    return x.T[:, :r]


def _transpose_cols(x):
    c = x.shape[1]
    if c % LANES:
        x = jnp.concatenate([x, jnp.zeros((x.shape[0], LANES - c % LANES), x.dtype)], axis=1)
    return x.T[:c, :]


def _in_proj_tn(d_model):
    return math.gcd(1024, 2 * d_model)


def _ffn_tc(d_ff):
    return _tile(d_ff, 512)


def _col_blocks(w, tn):
    k, m = w.shape
    return w.reshape(k, m // tn, tn).transpose(1, 0, 2)


def _params(n_axes, vmem=None):
    return pltpu.CompilerParams(
        dimension_semantics=("arbitrary",) * n_axes,
        vmem_limit_bytes=vmem,
    )


def _in_proj_kernel(x_ref, g_ref, w_ref, ws_ref, a_ref, s_ref, kv_ref, go_ref, xn_ref,
                    *, n_a, n_kv):
    j = pl.program_id(1)

    @pl.when(j == 0)
    def _():
        xn = _rms(x_ref[...], g_ref[...]).astype(BF16)
        xn_ref[...] = xn
        s_ref[...] = jnp.dot(xn, ws_ref[...], preferred_element_type=F32)

    def mm():
        return jnp.dot(xn_ref[...], w_ref[...], preferred_element_type=F32)

    @pl.when(j < n_a)
    def _():
        a_ref[...] = mm()

    @pl.when(jnp.logical_and(j >= n_a, j < n_a + n_kv))
    def _():
        kv_ref[...] = mm()

    @pl.when(j >= n_a + n_kv)
    def _():
        go_ref[...] = mm().astype(BF16)


def _in_proj(x2d, g, w_all, w_s, d_model):
    n, d = x2d.shape
    tm = _tile(n, 1024)
    tn = _in_proj_tn(d_model)
    assert w_all.shape[1:] == (d, tn)
    wa, wkv, wg = 2 * MLA_Q_RANK, 2 * FOX_WIDTH, 2 * d_model + FOX_WIDTH
    n_a, n_kv, n_g = wa // tn, wkv // tn, wg // tn
    ws = w_s.shape[1]
    kern = functools.partial(_in_proj_kernel, n_a=n_a, n_kv=n_kv)
    return pl.pallas_call(
        kern,
        grid=(n // tm, n_a + n_kv + n_g),
        in_specs=[
            pl.BlockSpec((tm, d), lambda i, j: (i, 0), pipeline_mode=pl.Buffered(1)),
            pl.BlockSpec((1, d), lambda i, j: (0, 0)),
            pl.BlockSpec((None, d, tn), lambda i, j: (j, 0, 0)),
            pl.BlockSpec((d, ws), lambda i, j: (0, 0)),
        ],
        out_specs=[
            pl.BlockSpec((tm, tn), lambda i, j: (i, jnp.minimum(j, n_a - 1))),
            pl.BlockSpec((tm, ws), lambda i, j: (i, 0)),
            pl.BlockSpec((tm, tn), lambda i, j: (i, jnp.clip(j - n_a, 0, n_kv - 1))),
            pl.BlockSpec((tm, tn), lambda i, j: (i, jnp.clip(j - n_a - n_kv, 0, n_g - 1))),
        ],
        out_shape=[
            jax.ShapeDtypeStruct((n, wa), F32),
            jax.ShapeDtypeStruct((n, ws), F32),
            jax.ShapeDtypeStruct((n, wkv), F32),
            jax.ShapeDtypeStruct((n, wg), BF16),
        ],
        scratch_shapes=[pltpu.VMEM((tm, d), BF16)],
        compiler_params=_params(2, V7X_VMEM_LIMIT),
        name="in_proj",
    )(x2d, g, w_all, w_s)


def _kv_up(ckvn, kr128, wukv_ref, kc_ref, v_ref):
    kv = jnp.dot(ckvn.astype(BF16), wukv_ref[...], preferred_element_type=F32)
    krb = kr128.astype(BF16)
    for h in range(HEADS):
        kc_ref[h, :, 0:HEAD_DIM] = kv[:, h * HEAD_DIM:(h + 1) * HEAD_DIM].astype(BF16)
        kc_ref[h, :, HEAD_DIM:QK_WIDTH] = krb
        v_ref[h, 0] = _transpose_rows(
            kv[:, (HEADS + h) * HEAD_DIM:(HEADS + h + 1) * HEAD_DIM]).astype(BF16)


def _mla_pre_kernel(a_ref, sm_ref, c_ref, s_ref, gq_ref, gkv_ref, bf_ref, wuq_ref, wukv_ref,
                    q_ref, ckv_ref, kr_ref, lf8_ref, lfp_ref, kc_ref, v_ref):
    a = a_ref[...]
    cos = c_ref[...]
    sin = s_ref[...]
    qn = _rms(a[:, :MLA_Q_RANK], gq_ref[...]).astype(BF16)
    q3 = jnp.dot(qn, wuq_ref[...], preferred_element_type=F32)
    hw = HEADS * HEAD_DIM
    for h in range(HEADS):
        lo, hi = h * HEAD_DIM, (h + 1) * HEAD_DIM
        q_ref[h, :, 0:HEAD_DIM] = q3[:, lo:hi].astype(BF16)
        rot = q3[:, hw + lo:hw + hi] * cos + q3[:, 2 * hw + lo:2 * hw + hi] * sin
        q_ref[h, :, HEAD_DIM:QK_WIDTH] = rot.astype(BF16)

    ckvn = _rms(a[:, MLA_Q_RANK:], gkv_ref[...])
    ckv_ref[...] = ckvn
    sm = sm_ref[...]
    kr128 = sm[:, 0:LANES] * cos + sm[:, LANES:2 * LANES] * sin
    kr_ref[...] = kr128[:, :MLA_ROPE]
    z = sm[:, 2 * LANES:3 * LANES] + bf_ref[...]
    lf = jnp.minimum(z, 0.0) - jnp.log1p(jnp.exp(-jnp.abs(z)))
    lane = lax.broadcasted_iota(jnp.int32, lf.shape, 1)
    lfp_ref[...] = jnp.where(lane < FOX_HEADS, lf, 0.0)
    lf8_ref[...] = lf[:, :FOX_HEADS]
    _kv_up(ckvn, kr128, wukv_ref, kc_ref, v_ref)


def _mla_pre(a3, sm3, cos_t, sin_t, gq, gkv, bf_pad, wuq3, wukv2):
    b, t, wa = a3.shape
    tm = _attn_tile(t)
    ws = sm3.shape[2]
    const = lambda bb, i: (0, 0)
    row3 = lambda bb, i: (bb, i, 0)
    head4 = lambda bb, i: (bb, 0, i, 0)
    vt5 = lambda bb, i: (bb, 0, i, 0, 0)
    return pl.pallas_call(
        _mla_pre_kernel,
        grid=(b, t // tm),
        in_specs=[
            pl.BlockSpec((None, tm, wa), row3),
            pl.BlockSpec((None, tm, ws), row3),
            pl.BlockSpec((tm, LANES), lambda bb, i: (i, 0)),
            pl.BlockSpec((tm, LANES), lambda bb, i: (i, 0)),
            pl.BlockSpec((1, MLA_Q_RANK), const),
            pl.BlockSpec((1, MLA_KV_RANK), const),
            pl.BlockSpec((1, LANES), const),
            pl.BlockSpec(wuq3.shape, const),
            pl.BlockSpec(wukv2.shape, const),
        ],
        out_specs=[
            pl.BlockSpec((None, HEADS, tm, QK_WIDTH), head4),
            pl.BlockSpec((None, tm, MLA_KV_RANK), row3),
            pl.BlockSpec((None, tm, MLA_ROPE), row3),
            pl.BlockSpec((None, tm, FOX_HEADS), row3),
            pl.BlockSpec((None, tm, LANES), row3),
            pl.BlockSpec((None, HEADS, tm, QK_WIDTH), head4),
            pl.BlockSpec((None, HEADS, 1, HEAD_DIM, tm), vt5),
        ],
        out_shape=[
            jax.ShapeDtypeStruct((b, HEADS, t, QK_WIDTH), BF16),
            jax.ShapeDtypeStruct((b, t, MLA_KV_RANK), F32),
            jax.ShapeDtypeStruct((b, t, MLA_ROPE), F32),
            jax.ShapeDtypeStruct((b, t, FOX_HEADS), F32),
            jax.ShapeDtypeStruct((b, t, LANES), F32),
            jax.ShapeDtypeStruct((b, HEADS, t, QK_WIDTH), BF16),
            jax.ShapeDtypeStruct((b, HEADS, t // tm, HEAD_DIM, tm), BF16),
        ],
        compiler_params=_params(2, V7X_VMEM_LIMIT),
        name="mla_pre",
    )(a3, sm3, cos_t, sin_t, gq, gkv, bf_pad, wuq3, wukv2)


def _kv_up_kernel(ckv_ref, kr_ref, wukv_ref, kc_ref, v_ref):
    kr = kr_ref[...]
    kr128 = jnp.concatenate([kr, jnp.zeros((kr.shape[0], LANES - MLA_ROPE), F32)], axis=1)
    _kv_up(ckv_ref[...], kr128, wukv_ref, kc_ref, v_ref)


def _kv_up_call(ckv3, kr3, wukv2):
    b, s, _ = ckv3.shape
    ts = _attn_tile(s)
    row3 = lambda bb, i: (bb, i, 0)
    head4 = lambda bb, i: (bb, 0, i, 0)
    return pl.pallas_call(
        _kv_up_kernel,
        grid=(b, s // ts),
        in_specs=[
            pl.BlockSpec((None, ts, MLA_KV_RANK), row3),
            pl.BlockSpec((None, ts, MLA_ROPE), row3),
            pl.BlockSpec(wukv2.shape, lambda bb, i: (0, 0)),
        ],
        out_specs=[
            pl.BlockSpec((None, HEADS, ts, QK_WIDTH), head4),
            pl.BlockSpec((None, HEADS, 1, HEAD_DIM, ts), lambda bb, i: (bb, 0, i, 0, 0)),
        ],
        out_shape=[
            jax.ShapeDtypeStruct((b, HEADS, s, QK_WIDTH), BF16),
            jax.ShapeDtypeStruct((b, HEADS, s // ts, HEAD_DIM, ts), BF16),
        ],
        compiler_params=_params(2, V7X_VMEM_LIMIT),
        name="kv_up",
    )(ckv3, kr3, wukv2)


def _fox_prep_kernel(*refs, has_q):
    if has_q:
        (q_ref, k_ref, v_ref, lf_ref, init_ref,
         qo_ref, ko_ref, vo_ref, last_ref, carry_ref) = refs
    else:
        (k_ref, v_ref, lf_ref, init_ref, ko_ref, vo_ref, last_ref, carry_ref) = refs
    i = pl.program_id(1)

    @pl.when(i == 0)
    def _():
        carry_ref[...] = init_ref[...]

    lf = lf_ref[...]
    ts = lf.shape[0]
    row = lax.broadcasted_iota(jnp.int32, (ts, ts), 0)
    col = lax.broadcasted_iota(jnp.int32, (ts, ts), 1)
    tri = jnp.where(col <= row, 1.0, 0.0).astype(BF16)
    cum = carry_ref[...]
    for part in _split3(lf):
        cum = cum + jnp.dot(tri, part.astype(BF16), preferred_element_type=F32)
    carry_ref[...] = cum[ts - 1:ts, :]
    last_ref[...] = cum[ts - 1:ts, :]

    c1, c2, c3 = _split3(cum * LOG2E)
    lane = lax.broadcasted_iota(jnp.int32, (ts, LANES), 1)
    k = k_ref[...]
    v = v_ref[...]
    for h in range(HEADS):
        lo, hi = h * HEAD_DIM, (h + 1) * HEAD_DIM
        h1, h2, h3 = c1[:, h:h + 1], c2[:, h:h + 1], c3[:, h:h + 1]
        ek = jnp.where(lane < 3, 1.0,
                       jnp.where(lane == 3, -h1,
                                 jnp.where(lane == 4, -h2,
                                           jnp.where(lane == 5, -h3, 0.0))))
        ko_ref[h, :, 0:HEAD_DIM] = k[:, lo:hi].astype(BF16)
        ko_ref[h, :, HEAD_DIM:QK_WIDTH] = ek.astype(BF16)
        vo_ref[h, 0] = _transpose_rows(v[:, lo:hi]).astype(BF16)
        if has_q:
            eq = jnp.where(lane == 0, h1,
                           jnp.where(lane == 1, h2,
                                     jnp.where(lane == 2, h3,
                                               jnp.where(lane < 6, 1.0, 0.0))))
            qo_ref[h, :, 0:HEAD_DIM] = q_ref[:, lo:hi]
            qo_ref[h, :, HEAD_DIM:QK_WIDTH] = eq.astype(BF16)


def _fox_prep(q_src, k_src, v_src, lf_pad, init):
    k_arr, k_blk = k_src
    v_arr, v_blk = v_src
    b, s, _ = k_arr.shape
    ts = _attn_tile(s)
    has_q = q_src is not None
    head4 = lambda bb, i: (bb, 0, i, 0)
    in_specs, args = [], []
    if has_q:
        q_arr, q_blk = q_src
        in_specs.append(pl.BlockSpec((None, ts, FOX_WIDTH), lambda bb, i: (bb, i, q_blk)))
        args.append(q_arr)
    in_specs += [
        pl.BlockSpec((None, ts, FOX_WIDTH), lambda bb, i: (bb, i, k_blk)),
        pl.BlockSpec((None, ts, FOX_WIDTH), lambda bb, i: (bb, i, v_blk)),
        pl.BlockSpec((None, ts, LANES), lambda bb, i: (bb, i, 0)),
        pl.BlockSpec((None, 1, LANES), lambda bb, i: (bb, 0, 0)),
    ]
    args += [k_arr, v_arr, lf_pad, init]
    out_specs, out_shape = [], []
    if has_q:
        out_specs.append(pl.BlockSpec((None, HEADS, ts, QK_WIDTH), head4))
        out_shape.append(jax.ShapeDtypeStruct((b, HEADS, s, QK_WIDTH), BF16))
    out_specs += [
        pl.BlockSpec((None, HEADS, ts, QK_WIDTH), head4),
        pl.BlockSpec((None, HEADS, 1, HEAD_DIM, ts), lambda bb, i: (bb, 0, i, 0, 0)),
        pl.BlockSpec((None, 1, LANES), lambda bb, i: (bb, 0, 0)),
    ]
    out_shape += [
        jax.ShapeDtypeStruct((b, HEADS, s, QK_WIDTH), BF16),
        jax.ShapeDtypeStruct((b, HEADS, s // ts, HEAD_DIM, ts), BF16),
        jax.ShapeDtypeStruct((b, 1, LANES), F32),
    ]
    return pl.pallas_call(
        functools.partial(_fox_prep_kernel, has_q=has_q),
        grid=(b, s // ts),
        in_specs=in_specs,
        out_specs=out_specs,
        out_shape=out_shape,
        scratch_shapes=[pltpu.VMEM((1, LANES), F32)],
        compiler_params=_params(2, V7X_VMEM_LIMIT),
        name="fox_prep_q" if has_q else "fox_prep_past",
    )(*args)


def _attn_kernel(*refs, tq, tkp, n_past, mask_shift, hp):
    if n_past:
        q_ref, kn_ref, vn_ref, kp_ref, vp_ref, o_ref = refs
    else:
        q_ref, kn_ref, vn_ref, o_ref = refs
    i = pl.program_id(2)

    def block(states, k_ref, off, tk, vt_ref, j, mask):
        scores = [lax.dot_general(k_ref[hh, pl.ds(off, tk), :], q_ref[hh],
                                  (((1,), (1,)), ((), ())), preferred_element_type=F32)
                  for hh in range(hp)]
        out = []
        for hh in range(hp):
            m_prev, l_prev, acc_prev = states[hh]
            s_t = scores[hh]
            if mask is not None:
                s_t = jnp.where(mask, s_t, NEG_INF)
            m_new = jnp.maximum(m_prev, jnp.max(s_t, axis=0, keepdims=True))
            alpha = jnp.exp2(m_prev - m_new)
            p_t = jnp.exp2(s_t - m_new)
            l_new = alpha * l_prev + jnp.sum(p_t, axis=0, keepdims=True)
            acc_new = alpha * acc_prev + jnp.dot(vt_ref[hh, j], p_t.astype(BF16),
                                                 preferred_element_type=F32)
            out.append((m_new, l_new, acc_new))
        return tuple(out)

    states = tuple((jnp.full((1, tq), NEG_INF, F32), jnp.zeros((1, tq), F32),
                    jnp.zeros((HEAD_DIM, tq), F32)) for _ in range(hp))

    if n_past:
        def past_body(j, st):
            return block(st, kp_ref, pl.multiple_of(j * tkp, tkp), tkp, vp_ref, j, None)
        states = lax.fori_loop(0, n_past, past_body, states)

    def new_body(j, st):
        return block(st, kn_ref, pl.multiple_of(j * tq, tq), tq, vn_ref, j, None)
    states = lax.fori_loop(0, i, new_body, states)

    key = lax.broadcasted_iota(jnp.int32, (tq, tq), 0)
    qry = lax.broadcasted_iota(jnp.int32, (tq, tq), 1)
    mask = (key >> mask_shift) <= (qry >> mask_shift)
    states = block(states, kn_ref, pl.multiple_of(i * tq, tq), tq, vn_ref, i, mask)
    for hh in range(hp):
        _, l_fin, acc_fin = states[hh]
        o_t = acc_fin / l_fin
        o_ref[:, hh * HEAD_DIM:(hh + 1) * HEAD_DIM] = _transpose_cols(o_t).astype(o_ref.dtype)


def _attention(q4, kn4, vn5, past, mask_shift):
    b, h, t, _ = q4.shape
    tq = _attn_tile(t)
    hp = ATTN_HEADS_PER_STEP
    assert h % hp == 0 and vn5.shape[4] == tq
    in_specs = [
        pl.BlockSpec((None, hp, tq, QK_WIDTH), lambda bb, g, i: (bb, g, i, 0)),
        pl.BlockSpec((None, hp, t, QK_WIDTH), lambda bb, g, i: (bb, g, 0, 0)),
        pl.BlockSpec((None, hp, t // tq, HEAD_DIM, tq), lambda bb, g, i: (bb, g, 0, 0, 0)),
    ]
    args = [q4, kn4, vn5]
    n_past, tkp = 0, 0
    if past is not None:
        kp4, vp5 = past
        p_len = kp4.shape[2]
        n_past, tkp = vp5.shape[2], vp5.shape[4]
        in_specs += [
            pl.BlockSpec((None, hp, p_len, QK_WIDTH), lambda bb, g, i: (bb, g, 0, 0)),
            pl.BlockSpec((None, hp, n_past, HEAD_DIM, tkp), lambda bb, g, i: (bb, g, 0, 0, 0)),
        ]
        args += [kp4, vp5]
    kern = functools.partial(_attn_kernel, tq=tq, tkp=tkp, n_past=n_past,
                             mask_shift=mask_shift, hp=hp)
    return pl.pallas_call(
        kern,
        grid=(b, h // hp, t // tq),
        in_specs=in_specs,
        out_specs=pl.BlockSpec((None, tq, hp * HEAD_DIM), lambda bb, g, i: (bb, i, g)),
        out_shape=jax.ShapeDtypeStruct((b, t, h * HEAD_DIM), BF16),
        compiler_params=_params(3, V7X_VMEM_LIMIT),
        name="attn_chunk" if mask_shift else "attn_frame",
    )(*args)


def _out_merge_kernel(oa_ref, ob_ref, ga_ref, gb_ref, x_ref, woa_ref, wob_ref, wo_ref,
                      h_ref, mg_ref, *, tc):
    d = x_ref.shape[1]
    oa = oa_ref[...]
    ob = ob_ref[...]
    for c in range(d // tc):
        sl = slice(c * tc, (c + 1) * tc)
        ta = jnp.dot(oa, woa_ref[:, sl], preferred_element_type=F32)
        tb = jnp.dot(ob, wob_ref[:, sl], preferred_element_type=F32)
        ga = jax.nn.sigmoid(ga_ref[:, sl].astype(F32))
        gb = jax.nn.sigmoid(gb_ref[:, sl].astype(F32))
        mg_ref[:, sl] = (ga * ta + gb * tb).astype(BF16)
    mg = mg_ref[...]
    for c in range(d // tc):
        sl = slice(c * tc, (c + 1) * tc)
        h_ref[:, sl] = x_ref[:, sl] + jnp.dot(mg, wo_ref[:, sl], preferred_element_type=F32)


def _out_merge(oa, ob, g_arr, x2d, woa, wob, wo):
    n, d = x2d.shape
    tm = _tile(n, 256)
    tc = _tile(d, 512)
    const = lambda i: (0, 0)
    one = pl.Buffered(1)
    return pl.pallas_call(
        functools.partial(_out_merge_kernel, tc=tc),
        grid=(n // tm,),
        in_specs=[
            pl.BlockSpec((tm, oa.shape[1]), lambda i: (i, 0)),
            pl.BlockSpec((tm, ob.shape[1]), lambda i: (i, 0)),
            pl.BlockSpec((tm, d), lambda i: (i, 0)),
            pl.BlockSpec((tm, d), lambda i: (i, 1)),
            pl.BlockSpec((tm, d), lambda i: (i, 0)),
            pl.BlockSpec(woa.shape, const, pipeline_mode=one),
            pl.BlockSpec(wob.shape, const, pipeline_mode=one),
            pl.BlockSpec(wo.shape, const, pipeline_mode=one),
        ],
        out_specs=pl.BlockSpec((tm, d), lambda i: (i, 0)),
        out_shape=jax.ShapeDtypeStruct((n, d), F32),
        scratch_shapes=[pltpu.VMEM((tm, d), BF16)],
        compiler_params=_params(1, V7X_VMEM_LIMIT),
        name="out_merge",
    )(oa, ob, g_arr, g_arr, x2d, woa, wob, wo)


def _ffn_kernel(h_ref, g_ref, wuv_ref, wug_ref, pv_ref, pg_ref, wcv_ref, wcg_ref,
                bcv_ref, bcg_ref, wd_ref, o_ref, ncv_ref, ncg_ref,
                hn_ref, acc_ref, act0_ref, act1_ref, cv_ref, cg_ref,
                *, seg, tiles_per_seq, nf):
    act_refs = (act0_ref, act1_ref)
    i = pl.program_id(0)
    j = pl.program_id(1)
    tm = h_ref.shape[0]
    nseg = tm // seg
    keep = CONV_W - 1
    seq_start = (i % tiles_per_seq) == 0

    def conv(w_ref, prev_ref, carry_ref, wc_ref, bc_ref, nc_ref):
        u = jnp.dot(hn_ref[...], w_ref[...], preferred_element_type=F32)
        wc = wc_ref[...]
        bc = bc_ref[...]
        outs = []
        for s in range(nseg):
            us = u[s * seg:(s + 1) * seg, :]
            tail = us[seg - keep:seg, :]
            if nseg == 1:
                halo = jnp.where(seq_start, prev_ref[0], carry_ref[j])
                carry_ref[j] = tail
            else:
                halo = prev_ref[s]
            nc_ref[s] = tail
            row = lax.broadcasted_iota(jnp.int32, us.shape, 0)
            u1 = jnp.where(row == 0, halo[1:2, :], pltpu.roll(us, 1, axis=0))
            u2 = jnp.where(row == 0, halo[0:1, :],
                           jnp.where(row == 1, halo[1:2, :], pltpu.roll(us, 2, axis=0)))
            outs.append(bc + u2 * wc[0:1, :] + u1 * wc[1:2, :] + us * wc[2:3, :])
        return outs

    def up(slot):
        vals = conv(wuv_ref, pv_ref, cv_ref, wcv_ref, bcv_ref, ncv_ref)
        gates = conv(wug_ref, pg_ref, cg_ref, wcg_ref, bcg_ref, ncg_ref)
        for s in range(nseg):
            gate = gates[s]
            cdf = 0.5 * (1.0 + jnp.tanh(
                0.7978845608028654 * (gate + 0.044715 * (gate * gate * gate))))
            act_refs[slot][s * seg:(s + 1) * seg, :] = (gate * cdf * vals[s]).astype(BF16)

    def down(slot):
        acc_ref[...] += jnp.dot(act_refs[slot][...], wd_ref[...], preferred_element_type=F32)

    @pl.when(j == 0)
    def _():
        hn_ref[...] = _rms(h_ref[...], g_ref[...]).astype(BF16)
        acc_ref[...] = jnp.zeros(acc_ref.shape, F32)
        up(0)

    for parity in range(2):
        @pl.when(jnp.logical_and(jnp.logical_and(j >= 1, j < nf), j % 2 == parity))
        def _():
            up(parity)
            down(1 - parity)

    @pl.when(j == nf)
    def _():
        down((nf - 1) % 2)
        o_ref[...] = h_ref[...] + acc_ref[...]


def _ffn(h2d, prev, g, w_up, w_conv, b_conv, w_down, t):
    n, d = h2d.shape
    f = w_down.shape[0]
    b = n // t
    keep = CONV_W - 1
    assert CONV_W == 3
    tm = _tile(n, 512)
    if t >= tm:
        tm = _tile(t, 512)
        seg, nb_t, tiles_per_seq = tm, 1, t // tm
        bidx = lambda i: i // tiles_per_seq
    else:
        assert tm % t == 0
        seg, nb_t, tiles_per_seq = t, tm // t, 1
        bidx = lambda i: i
    tc = _ffn_tc(f)
    nf = f // tc
    assert w_up.shape == (2 * nf, d, tc)
    up_j = lambda j: jnp.minimum(j, nf - 1)
    down_j = lambda j: jnp.maximum(j - 1, 0)
    h2, ncv, ncg = pl.pallas_call(
        functools.partial(_ffn_kernel, seg=seg, tiles_per_seq=tiles_per_seq, nf=nf),
        grid=(n // tm, nf + 1),
        in_specs=[
            pl.BlockSpec((tm, d), lambda i, j: (i, 0)),
            pl.BlockSpec((1, d), lambda i, j: (0, 0)),
            pl.BlockSpec((None, d, tc), lambda i, j: (up_j(j), 0, 0)),
            pl.BlockSpec((None, d, tc), lambda i, j: (up_j(j) + nf, 0, 0)),
            pl.BlockSpec((nb_t, keep, tc), lambda i, j: (bidx(i), 0, up_j(j))),
            pl.BlockSpec((nb_t, keep, tc), lambda i, j: (bidx(i), 0, up_j(j) + nf)),
            pl.BlockSpec((CONV_W, tc), lambda i, j: (0, up_j(j))),
            pl.BlockSpec((CONV_W, tc), lambda i, j: (0, up_j(j) + nf)),
            pl.BlockSpec((1, tc), lambda i, j: (0, up_j(j))),
            pl.BlockSpec((1, tc), lambda i, j: (0, up_j(j) + nf)),
            pl.BlockSpec((tc, d), lambda i, j: (down_j(j), 0)),
        ],
        out_specs=[
            pl.BlockSpec((tm, d), lambda i, j: (i, 0)),
            pl.BlockSpec((nb_t, keep, tc), lambda i, j: (i, 0, up_j(j))),
            pl.BlockSpec((nb_t, keep, tc), lambda i, j: (i, 0, up_j(j))),
        ],
        out_shape=[
            jax.ShapeDtypeStruct((n, d), F32),
            jax.ShapeDtypeStruct((b * tiles_per_seq, keep, f), F32),
            jax.ShapeDtypeStruct((b * tiles_per_seq, keep, f), F32),
        ],
        scratch_shapes=[
            pltpu.VMEM((tm, d), BF16),
            pltpu.VMEM((tm, d), F32),
            pltpu.VMEM((tm, tc), BF16),
            pltpu.VMEM((tm, tc), BF16),
            pltpu.VMEM((nf, keep, tc), F32),
            pltpu.VMEM((nf, keep, tc), F32),
        ],
        compiler_params=_params(2, V7X_VMEM_LIMIT),
        name="ffn",
    )(h2d, g, w_up, w_up, prev, prev, w_conv, w_conv, b_conv, b_conv, w_down)
    last = lambda a: a.reshape(b, tiles_per_seq, keep, f)[:, tiles_per_seq - 1]
    return h2, jnp.concatenate([last(ncv), last(ncg)], axis=-1)


def _ple_final_kernel(h_ref, pe_ref, gp_ref, gf_ref, wpg_ref, wple_ref, y_ref, h3_ref, *, tc):
    d = h_ref.shape[1]
    hn = _rms(h_ref[...], gp_ref[...]).astype(BF16)
    pe = pe_ref[...].astype(BF16)
    for c in range(d // tc):
        sl = slice(c * tc, (c + 1) * tc)
        gate = jax.nn.sigmoid(jnp.dot(hn, wpg_ref[:, sl], preferred_element_type=F32))
        val = jnp.dot(pe, wple_ref[:, sl], preferred_element_type=F32)
        h3_ref[:, sl] = h_ref[:, sl] + gate * val
    y_ref[...] = _rms(h3_ref[...], gf_ref[...])


def _ple_final(h2d, pe2d, g_ple, g_final, wpg, wple):
    n, d = h2d.shape
    tm = _tile(n, 256)
    tc = _tile(d, 512)
    const = lambda i: (0, 0)
    one = pl.Buffered(1)
    return pl.pallas_call(
        functools.partial(_ple_final_kernel, tc=tc),
        grid=(n // tm,),
        in_specs=[
            pl.BlockSpec((tm, d), lambda i: (i, 0)),
            pl.BlockSpec((tm, pe2d.shape[1]), lambda i: (i, 0)),
            pl.BlockSpec((1, d), const),
            pl.BlockSpec((1, d), const),
            pl.BlockSpec(wpg.shape, const, pipeline_mode=one),
            pl.BlockSpec(wple.shape, const, pipeline_mode=one),
        ],
        out_specs=pl.BlockSpec((tm, d), lambda i: (i, 0)),
        out_shape=jax.ShapeDtypeStruct((n, d), F32),
        scratch_shapes=[pltpu.VMEM((tm, d), F32)],
        compiler_params=_params(1, V7X_VMEM_LIMIT),
        name="ple_final",
    )(h2d, pe2d, g_ple, g_final, wpg, wple)


def _prep_weights(w, d_model):
    half = MLA_ROPE // 2
    o_cq = 0
    o_ckv = o_cq + MLA_Q_RANK
    o_kr = o_ckv + MLA_KV_RANK
    o_fq = o_kr + MLA_ROPE
    o_fk = o_fq + FOX_WIDTH
    o_fv = o_fk + FOX_WIDTH
    o_fl = o_fv + FOX_WIDTH
    o_ga = o_fl + FOX_HEADS
    o_gb = o_ga + d_model
    w_in = w["w_in"]
    d = w_in.shape[0]
    col = lambda o, n: w_in[:, o:o + n]
    w_all = jnp.concatenate([
        col(o_cq, MLA_Q_RANK), col(o_ckv, MLA_KV_RANK),
        col(o_fk, FOX_WIDTH), col(o_fv, FOX_WIDTH),
        col(o_ga, d_model), col(o_gb, d_model),
        col(o_fq, FOX_WIDTH) * (FOX_SCALE * LOG2E),
    ], axis=1).astype(BF16)
    zeros = lambda n: jnp.zeros((d, n), F32)
    w_s = jnp.concatenate([
        col(o_kr, MLA_ROPE), zeros(LANES - MLA_ROPE),
        col(o_kr + half, half), col(o_kr, half), zeros(LANES - MLA_ROPE),
        col(o_fl, FOX_HEADS), zeros(LANES - FOX_HEADS),
    ], axis=1).astype(BF16)

    qk = MLA_NOPE + MLA_ROPE
    wq = w["w_uq"].reshape(MLA_Q_RANK, HEADS, qk) * (MLA_SCALE * LOG2E)
    pad = jnp.zeros((MLA_Q_RANK, HEADS, LANES - MLA_ROPE), F32)
    nope = wq[:, :, :MLA_NOPE]
    x1 = wq[:, :, MLA_NOPE:MLA_NOPE + half]
    x2 = wq[:, :, MLA_NOPE + half:]
    rot = jnp.concatenate([x1, x2, pad], axis=2)
    rot_sw = jnp.concatenate([x2, x1, pad], axis=2)
    wuq3 = jnp.concatenate([nope.reshape(MLA_Q_RANK, -1), rot.reshape(MLA_Q_RANK, -1),
                            rot_sw.reshape(MLA_Q_RANK, -1)], axis=1).astype(BF16)
    wkv = w["w_ukv"].reshape(MLA_KV_RANK, HEADS, MLA_NOPE + MLA_V)
    wukv2 = jnp.concatenate([wkv[:, :, :MLA_NOPE].reshape(MLA_KV_RANK, -1),
                             wkv[:, :, MLA_NOPE:].reshape(MLA_KV_RANK, -1)], axis=1).astype(BF16)
    bf_pad = jnp.zeros((1, LANES), F32).at[0, :FOX_HEADS].set(w["b_f"])
    w_all = _col_blocks(w_all, _in_proj_tn(d_model))
    w_up = _col_blocks(w["w_up"].astype(BF16), _ffn_tc(w["w_down"].shape[0]))
    return dict(
        w_all=w_all, w_s=w_s, wuq3=wuq3, wukv2=wukv2, bf_pad=bf_pad,
        g_mix=w["g_mix"][None, :], g_q=w["g_q"][None, :], g_kv=w["g_kv"][None, :],
        w_oa=w["w_oa"].astype(BF16), w_ob=w["w_ob"].astype(BF16), w_o=w["w_o"].astype(BF16),
        g_ffn=w["g_ffn"][None, :], w_up=w_up,
        w_conv=w["w_conv"], b_conv=w["b_conv"][None, :], w_down=w["w_down"].astype(BF16),
        g_ple=w["g_ple"][None, :], w_pg=w["w_pg"].astype(BF16), w_ple=w["w_ple"].astype(BF16),
    )


def _rope_tables(pos):
    half = MLA_ROPE // 2
    inv = ROPE_THETA ** (-jnp.arange(half, dtype=F32) / half)
    ang = pos.astype(F32)[:, None] * inv[None, :]
    cos, sin = jnp.cos(ang), jnp.sin(ang)
    zero = jnp.zeros((pos.shape[0], LANES - MLA_ROPE), F32)
    return (jnp.concatenate([cos, cos, zero], axis=1),
            jnp.concatenate([-sin, sin, zero], axis=1))


def _layer(x, pe, past, pw, g_final):
    b, t, d = x.shape
    n = b * t
    p_len = 0 if past is None else past[0].shape[1]
    x2d = x.reshape(n, d)

    a2d, sm2d, kv2d, g2d = _in_proj(x2d, pw["g_mix"], pw["w_all"], pw["w_s"], d)
    cos_t, sin_t = _rope_tables(p_len + jnp.arange(t))
    (q_mla, ckv_n, kr_n, logf, logf_pad, kc_new, v_new) = _mla_pre(
        a2d.reshape(b, t, -1), sm2d.reshape(b, t, -1), cos_t, sin_t,
        pw["g_q"], pw["g_kv"], pw["bf_pad"], pw["wuq3"], pw["wukv2"])

    kv3 = kv2d.reshape(b, t, -1)
    g3 = g2d.reshape(b, t, -1)
    fq_blk = (2 * d) // FOX_WIDTH
    assert (2 * d) % FOX_WIDTH == 0
    if past is None:
        mla_past = None
        fox_past = None
        init = jnp.zeros((b, 1, LANES), F32)
    else:
        c_ckv, c_kr, c_fk, c_fv, c_logf, _ = past
        mla_past = _kv_up_call(c_ckv, c_kr, pw["wukv2"])
        lf_past = jnp.pad(c_logf, ((0, 0), (0, 0), (0, LANES - FOX_HEADS)))
        kp, vp, init = _fox_prep(None, (c_fk.reshape(b, p_len, -1), 0),
                                 (c_fv.reshape(b, p_len, -1), 0), lf_past,
                                 jnp.zeros((b, 1, LANES), F32))
        fox_past = (kp, vp)
    q_fox, k_fox, v_fox, _ = _fox_prep((g3, fq_blk), (kv3, 0), (kv3, 1), logf_pad, init)

    o_a = _attention(q_mla, kc_new, v_new, mla_past, int(math.log2(CHUNK)))
    o_b = _attention(q_fox, k_fox, v_fox, fox_past, 0)

    h2d = _out_merge(o_a.reshape(n, -1), o_b.reshape(n, -1), g2d, x2d,
                     pw["w_oa"], pw["w_ob"], pw["w_o"])

    f2 = 2 * pw["w_down"].shape[0]
    prev = (jnp.zeros((b, CONV_W - 1, f2), F32) if past is None else past[5])
    h2d, new_conv = _ffn(h2d, prev, pw["g_ffn"], pw["w_up"], pw["w_conv"], pw["b_conv"],
                         pw["w_down"], t)

    y2d = _ple_final(h2d, pe.reshape(n, -1), pw["g_ple"], g_final[None, :], pw["w_pg"], pw["w_ple"])

    fk = kv3[:, :, :FOX_WIDTH].reshape(b, t, FOX_HEADS, FOX_HEAD_DIM)
    fv = kv3[:, :, FOX_WIDTH:].reshape(b, t, FOX_HEADS, FOX_HEAD_DIM)
    return y2d.reshape(b, t, d), (ckv_n, kr_n, fk, fv, logf, new_conv)


def kernel(x_prompt, x_sample, cache_mla_ckv, cache_mla_krope, cache_fox_k, cache_fox_v,
           cache_fox_logf, state_ffn_conv, p_prompt, p_sample, g_mix, w_in, b_f, g_q, w_uq,
           g_kv, w_ukv, w_oa, w_ob, w_o, g_ffn, w_up, w_conv, b_conv, w_down, g_ple, w_pg,
           w_ple, g_final):
    depth = w_in.shape[0]
    assert depth == 1, "the final norm is fused into the layer's last kernel"
    d_model = x_prompt.shape[-1]
    w = {"g_mix": g_mix[0], "w_in": w_in[0], "b_f": b_f[0], "g_q": g_q[0], "w_uq": w_uq[0],
         "g_kv": g_kv[0], "w_ukv": w_ukv[0], "w_oa": w_oa[0], "w_ob": w_ob[0], "w_o": w_o[0],
         "g_ffn": g_ffn[0], "w_up": w_up[0], "w_conv": w_conv[0], "b_conv": b_conv[0],
         "w_down": w_down[0], "g_ple": g_ple[0], "w_pg": w_pg[0], "w_ple": w_ple[0]}
    pw = _prep_weights(w, d_model)

    y_p, st_p = _layer(x_prompt, p_prompt[0], None, pw, g_final)
    past = (cache_mla_ckv[0], cache_mla_krope[0], cache_fox_k[0], cache_fox_v[0],
            cache_fox_logf[0], state_ffn_conv[0])
    y_s, st_s = _layer(x_sample, p_sample[0], past, pw, g_final)

    outs = [y_p, y_s]
    for j in range(6):
        outs.append(st_p[j][None])
        outs.append(st_s[j][None])
    return tuple(outs)
```

```python
import functools
import math

import jax
import jax.numpy as jnp
from jax import lax
from jax.experimental import pallas as pl
from jax.experimental.pallas import tpu as pltpu

F32 = jnp.float32
BF16 = jnp.bfloat16

CHUNK = 64
MLA_HEADS = 8
MLA_Q_RANK = 512
MLA_KV_RANK = 512
MLA_NOPE = 128
MLA_ROPE = 64
MLA_V = 128
MLA_SCALE = (MLA_NOPE + MLA_ROPE) ** -0.5
ROPE_THETA = 10000.0
FOX_HEADS = 8
FOX_HEAD_DIM = 128
FOX_WIDTH = FOX_HEADS * FOX_HEAD_DIM
FOX_SCALE = FOX_HEAD_DIM ** -0.5
CONV_W = 3
EPS = 1e-6
NEG_INF = -1e30
LOG2E = 1.4426950408889634

HEADS = 8
HEAD_DIM = 128
QK_WIDTH = 256
LANES = 128
SUBLANES = 8
GELU_C = 0.7978845608028654
GELU_A = 0.044715
ATTN_HEADS_PER_STEP = 4
V7X_VMEM_LIMIT = 56 * 1024 * 1024

assert MLA_HEADS == HEADS and FOX_HEADS == HEADS
assert MLA_NOPE == HEAD_DIM and MLA_V == HEAD_DIM and FOX_HEAD_DIM == HEAD_DIM


def _tile(n, pref):
    if n <= pref:
        return n
    for t in range(pref, 7, -1):
        if n % t == 0 and t % 8 == 0:
            return t
    return n


def _rms(x, g):
    ms = jnp.mean(x * x, axis=-1, keepdims=True)
    return x * lax.rsqrt(ms + EPS) * g


def _split3(x):
    a1 = x.astype(BF16).astype(F32)
    r1 = x - a1
    a2 = r1.astype(BF16).astype(F32)
    a3 = (r1 - a2).astype(BF16).astype(F32)
    return a1, a2, a3


def _attn_tile(t):
    return _tile(t, 512)


def _transpose_rows(x):
    r = x.shape[0]
    if r % LANES:
        x = jnp.concatenate([x, jnp.zeros((LANES - r % LANES, x.shape[1]), x.dtype)], axis=0)
    return x.T[:, :r]


def _transpose_cols(x):
    c = x.shape[1]
    if c % LANES:
        x = jnp.concatenate([x, jnp.zeros((x.shape[0], LANES - c % LANES), x.dtype)], axis=1)
    return x.T[:c, :]


def _in_proj_tn(d_model):
    return math.gcd(1024, 2 * d_model)


def _ffn_tc(d_ff):
    return _tile(d_ff, 512)


def _params(n_axes, vmem=None, flags=None):
    return pltpu.CompilerParams(
        dimension_semantics=("arbitrary",) * n_axes,
        vmem_limit_bytes=vmem,
        flags=flags,
    )


def _in_proj_kernel(x_ref, g_ref, w_ref, ws_ref, s_ref, *rest, bounds):
    out_refs, xn_ref = rest[:-1], rest[-1]
    j = pl.program_id(1)

    @pl.when(j == 0)
    def _():
        xn = _rms(x_ref[...], g_ref[...]).astype(BF16)
        xn_ref[...] = xn
        s_ref[...] = jnp.dot(xn, ws_ref[...], preferred_element_type=F32)

    for o_ref, (lo, hi) in zip(out_refs, bounds):
        @pl.when(jnp.logical_and(j >= lo, j < hi))
        def _(o_ref=o_ref):
            o_ref[...] = jnp.dot(xn_ref[...], w_ref[...],
                                 preferred_element_type=F32).astype(o_ref.dtype)


def _in_proj(x2d, g, w_all, w_s, d_model):
    n, d = x2d.shape
    tm = _tile(n, 512)
    tn = _in_proj_tn(d_model)
    groups = [(2 * MLA_Q_RANK, F32), (FOX_WIDTH, F32), (FOX_WIDTH, F32),
              (2 * d_model + FOX_WIDTH, BF16)]
    assert w_all.shape == (d, sum(wd for wd, _ in groups))
    bounds, lo = [], 0
    for wd, _ in groups:
        bounds.append((lo, lo + wd // tn))
        lo += wd // tn
    ws = w_s.shape[1]

    def group_spec(lo, hi):
        return pl.BlockSpec((tm, tn), lambda i, j: (i, jnp.clip(j - lo, 0, hi - lo - 1)))

    return pl.pallas_call(
        functools.partial(_in_proj_kernel, bounds=tuple(bounds)),
        grid=(n // tm, lo),
        in_specs=[
            pl.BlockSpec((tm, d), lambda i, j: (i, 0)),
            pl.BlockSpec((1, d), lambda i, j: (0, 0)),
            pl.BlockSpec((d, tn), lambda i, j: (0, j)),
            pl.BlockSpec((d, ws), lambda i, j: (0, 0)),
        ],
        out_specs=[pl.BlockSpec((tm, ws), lambda i, j: (i, 0))]
        + [group_spec(lo_, hi_) for lo_, hi_ in bounds],
        out_shape=[jax.ShapeDtypeStruct((n, ws), F32)]
        + [jax.ShapeDtypeStruct((n, wd), dt) for wd, dt in groups],
        scratch_shapes=[pltpu.VMEM((tm, d), BF16)],
        compiler_params=_params(2, V7X_VMEM_LIMIT),
        name="in_proj",
    )(x2d, g, w_all, w_s)


def _kv_up(ckvn, kr128, wukv_ref, kc_ref, v_ref):
    kv = jnp.dot(ckvn.astype(BF16), wukv_ref[...], preferred_element_type=F32)
    krb = kr128.astype(BF16)
    for h in range(HEADS):
        kc_ref[h, :, 0:HEAD_DIM] = kv[:, h * HEAD_DIM:(h + 1) * HEAD_DIM].astype(BF16)
        kc_ref[h, :, HEAD_DIM:QK_WIDTH] = krb
        v_ref[h, 0] = _transpose_rows(
            kv[:, (HEADS + h) * HEAD_DIM:(HEADS + h + 1) * HEAD_DIM]).astype(BF16)


def _mla_pre_kernel(a_ref, sm_ref, c_ref, s_ref, gq_ref, gkv_ref, bf_ref, wuq_ref, wukv_ref,
                    q_ref, ckv_ref, kr_ref, lf8_ref, lfp_ref, kc_ref, v_ref):
    a = a_ref[...]
    cos = c_ref[...]
    sin = s_ref[...]
    qn = _rms(a[:, :MLA_Q_RANK], gq_ref[...]).astype(BF16)
    q3 = jnp.dot(qn, wuq_ref[...], preferred_element_type=F32)
    hw = HEADS * HEAD_DIM
    for h in range(HEADS):
        lo, hi = h * HEAD_DIM, (h + 1) * HEAD_DIM
        q_ref[h, :, 0:HEAD_DIM] = q3[:, lo:hi].astype(BF16)
        rot = q3[:, hw + lo:hw + hi] * cos + q3[:, 2 * hw + lo:2 * hw + hi] * sin
        q_ref[h, :, HEAD_DIM:QK_WIDTH] = rot.astype(BF16)

    ckvn = _rms(a[:, MLA_Q_RANK:], gkv_ref[...])
    ckv_ref[...] = ckvn
    sm = sm_ref[...]
    kr128 = sm[:, 0:LANES] * cos + sm[:, LANES:2 * LANES] * sin
    kr_ref[...] = kr128[:, :MLA_ROPE]
    z = sm[:, 2 * LANES:3 * LANES] + bf_ref[...]
    lf = jnp.minimum(z, 0.0) - jnp.log1p(jnp.exp(-jnp.abs(z)))
    lane = lax.broadcasted_iota(jnp.int32, lf.shape, 1)
    lfp_ref[...] = jnp.where(lane < FOX_HEADS, lf, 0.0)
    lf8_ref[...] = lf[:, :FOX_HEADS]
    _kv_up(ckvn, kr128, wukv_ref, kc_ref, v_ref)


def _mla_pre(a3, sm3, cos_t, sin_t, gq, gkv, bf_pad, wuq3, wukv2):
    b, t, wa = a3.shape
    tm = _attn_tile(t)
    ws = sm3.shape[2]
    const = lambda bb, i: (0, 0)
    row3 = lambda bb, i: (bb, i, 0)
    head4 = lambda bb, i: (bb, 0, i, 0)
    vt5 = lambda bb, i: (bb, 0, i, 0, 0)
    return pl.pallas_call(
        _mla_pre_kernel,
        grid=(b, t // tm),
        in_specs=[
            pl.BlockSpec((None, tm, wa), row3),
            pl.BlockSpec((None, tm, ws), row3),
            pl.BlockSpec((tm, LANES), lambda bb, i: (i, 0)),
            pl.BlockSpec((tm, LANES), lambda bb, i: (i, 0)),
            pl.BlockSpec((1, MLA_Q_RANK), const),
            pl.BlockSpec((1, MLA_KV_RANK), const),
            pl.BlockSpec((1, LANES), const),
            pl.BlockSpec(wuq3.shape, const),
            pl.BlockSpec(wukv2.shape, const),
        ],
        out_specs=[
            pl.BlockSpec((None, HEADS, tm, QK_WIDTH), head4),
            pl.BlockSpec((None, tm, MLA_KV_RANK), row3),
            pl.BlockSpec((None, tm, MLA_ROPE), row3),
            pl.BlockSpec((None, tm, FOX_HEADS), row3),
            pl.BlockSpec((None, tm, LANES), row3),
            pl.BlockSpec((None, HEADS, tm, QK_WIDTH), head4),
            pl.BlockSpec((None, HEADS, 1, HEAD_DIM, tm), vt5),
        ],
        out_shape=[
            jax.ShapeDtypeStruct((b, HEADS, t, QK_WIDTH), BF16),
            jax.ShapeDtypeStruct((b, t, MLA_KV_RANK), F32),
            jax.ShapeDtypeStruct((b, t, MLA_ROPE), F32),
            jax.ShapeDtypeStruct((b, t, FOX_HEADS), F32),
            jax.ShapeDtypeStruct((b, t, LANES), F32),
            jax.ShapeDtypeStruct((b, HEADS, t, QK_WIDTH), BF16),
            jax.ShapeDtypeStruct((b, HEADS, t // tm, HEAD_DIM, tm), BF16),
        ],
        compiler_params=_params(2, V7X_VMEM_LIMIT),
        name="mla_pre",
    )(a3, sm3, cos_t, sin_t, gq, gkv, bf_pad, wuq3, wukv2)


def _kv_up_kernel(ckv_ref, kr_ref, wukv_ref, kc_ref, v_ref):
    kr = kr_ref[...]
    kr128 = jnp.concatenate([kr, jnp.zeros((kr.shape[0], LANES - MLA_ROPE), F32)], axis=1)
    _kv_up(ckv_ref[...], kr128, wukv_ref, kc_ref, v_ref)


def _kv_up_call(ckv3, kr3, wukv2):
    b, s, _ = ckv3.shape
    ts = _attn_tile(s)
    row3 = lambda bb, i: (bb, i, 0)
    head4 = lambda bb, i: (bb, 0, i, 0)
    return pl.pallas_call(
        _kv_up_kernel,
        grid=(b, s // ts),
        in_specs=[
            pl.BlockSpec((None, ts, MLA_KV_RANK), row3),
            pl.BlockSpec((None, ts, MLA_ROPE), row3),
            pl.BlockSpec(wukv2.shape, lambda bb, i: (0, 0)),
        ],
        out_specs=[
            pl.BlockSpec((None, HEADS, ts, QK_WIDTH), head4),
            pl.BlockSpec((None, HEADS, 1, HEAD_DIM, ts), lambda bb, i: (bb, 0, i, 0, 0)),
        ],
        out_shape=[
            jax.ShapeDtypeStruct((b, HEADS, s, QK_WIDTH), BF16),
            jax.ShapeDtypeStruct((b, HEADS, s // ts, HEAD_DIM, ts), BF16),
        ],
        compiler_params=_params(2, V7X_VMEM_LIMIT),
        name="kv_up",
    )(ckv3, kr3, wukv2)


def _fox_prep_kernel(*refs, has_q):
    if has_q:
        (q_ref, k_ref, v_ref, lf_ref, init_ref,
         qo_ref, ko_ref, vo_ref, last_ref, carry_ref) = refs
    else:
        (k_ref, v_ref, lf_ref, init_ref, ko_ref, vo_ref, last_ref, carry_ref) = refs
    i = pl.program_id(1)

    @pl.when(i == 0)
    def _():
        carry_ref[...] = init_ref[...]

    lf = lf_ref[...]
    ts = lf.shape[0]
    row = lax.broadcasted_iota(jnp.int32, (ts, ts), 0)
    col = lax.broadcasted_iota(jnp.int32, (ts, ts), 1)
    tri = jnp.where(col <= row, 1.0, 0.0).astype(BF16)
    cum = carry_ref[...]
    for part in _split3(lf):
        cum = cum + jnp.dot(tri, part.astype(BF16), preferred_element_type=F32)
    carry_ref[...] = cum[ts - 1:ts, :]
    last_ref[...] = cum[ts - 1:ts, :]

    c1, c2, c3 = _split3(cum * LOG2E)
    lane = lax.broadcasted_iota(jnp.int32, (ts, LANES), 1)
    k = k_ref[...]
    v = v_ref[...]
    for h in range(HEADS):
        lo, hi = h * HEAD_DIM, (h + 1) * HEAD_DIM
        h1, h2, h3 = c1[:, h:h + 1], c2[:, h:h + 1], c3[:, h:h + 1]
        ek = jnp.where(lane < 3, 1.0,
                       jnp.where(lane == 3, -h1,
                                 jnp.where(lane == 4, -h2,
                                           jnp.where(lane == 5, -h3, 0.0))))
        ko_ref[h, :, 0:HEAD_DIM] = k[:, lo:hi].astype(BF16)
        ko_ref[h, :, HEAD_DIM:QK_WIDTH] = ek.astype(BF16)
        vo_ref[h, 0] = _transpose_rows(v[:, lo:hi]).astype(BF16)
        if has_q:
            eq = jnp.where(lane == 0, h1,
                           jnp.where(lane == 1, h2,
                                     jnp.where(lane == 2, h3,
                                               jnp.where(lane < 6, 1.0, 0.0))))
            qo_ref[h, :, 0:HEAD_DIM] = q_ref[:, lo:hi]
            qo_ref[h, :, HEAD_DIM:QK_WIDTH] = eq.astype(BF16)


def _fox_prep(q_src, k_src, v_src, lf_pad, init):
    k_arr, k_blk = k_src
    v_arr, v_blk = v_src
    b, s, _ = k_arr.shape
    ts = _attn_tile(s)
    has_q = q_src is not None
    head4 = lambda bb, i: (bb, 0, i, 0)
    in_specs, args = [], []
    if has_q:
        q_arr, q_blk = q_src
        in_specs.append(pl.BlockSpec((None, ts, FOX_WIDTH), lambda bb, i: (bb, i, q_blk)))
        args.append(q_arr)
    in_specs += [
        pl.BlockSpec((None, ts, FOX_WIDTH), lambda bb, i: (bb, i, k_blk)),
        pl.BlockSpec((None, ts, FOX_WIDTH), lambda bb, i: (bb, i, v_blk)),
        pl.BlockSpec((None, ts, LANES), lambda bb, i: (bb, i, 0)),
        pl.BlockSpec((None, 1, LANES), lambda bb, i: (bb, 0, 0)),
    ]
    args += [k_arr, v_arr, lf_pad, init]
    out_specs, out_shape = [], []
    if has_q:
        out_specs.append(pl.BlockSpec((None, HEADS, ts, QK_WIDTH), head4))
        out_shape.append(jax.ShapeDtypeStruct((b, HEADS, s, QK_WIDTH), BF16))
    out_specs += [
        pl.BlockSpec((None, HEADS, ts, QK_WIDTH), head4),
        pl.BlockSpec((None, HEADS, 1, HEAD_DIM, ts), lambda bb, i: (bb, 0, i, 0, 0)),
        pl.BlockSpec((None, 1, LANES), lambda bb, i: (bb, 0, 0)),
    ]
    out_shape += [
        jax.ShapeDtypeStruct((b, HEADS, s, QK_WIDTH), BF16),
        jax.ShapeDtypeStruct((b, HEADS, s // ts, HEAD_DIM, ts), BF16),
        jax.ShapeDtypeStruct((b, 1, LANES), F32),
    ]
    return pl.pallas_call(
        functools.partial(_fox_prep_kernel, has_q=has_q),
        grid=(b, s // ts),
        in_specs=in_specs,
        out_specs=out_specs,
        out_shape=out_shape,
        scratch_shapes=[pltpu.VMEM((1, LANES), F32)],
        compiler_params=_params(2, V7X_VMEM_LIMIT),
        name="fox_prep_q" if has_q else "fox_prep_past",
    )(*args)


def _attn_kernel(*refs, tq, tkp, n_past, mask_shift, hp):
    if n_past:
        q_ref, kn_ref, vn_ref, kp_ref, vp_ref, o_ref = refs
    else:
        q_ref, kn_ref, vn_ref, o_ref = refs
    i = pl.program_id(2)

    def block(states, k_ref, off, tk, vt_ref, j, mask):
        scores = [lax.dot_general(k_ref[hh, pl.ds(off, tk), :], q_ref[hh],
                                  (((1,), (1,)), ((), ())), preferred_element_type=F32)
                  for hh in range(hp)]
        out = []
        for hh in range(hp):
            m_prev, l_prev, acc_prev = states[hh]
            s_t = scores[hh]
            if mask is not None:
                s_t = jnp.where(mask, s_t, NEG_INF)
            m_new = jnp.maximum(m_prev, jnp.max(s_t, axis=0, keepdims=True))
            alpha = jnp.exp2(m_prev - m_new)
            p_t = jnp.exp2(s_t - m_new)
            l_new = alpha * l_prev + jnp.sum(p_t, axis=0, keepdims=True)
            acc_new = alpha * acc_prev + jnp.dot(vt_ref[hh, j], p_t.astype(BF16),
                                                 preferred_element_type=F32)
            out.append((m_new, l_new, acc_new))
        return tuple(out)

    states = tuple((jnp.full((1, tq), NEG_INF, F32), jnp.zeros((1, tq), F32),
                    jnp.zeros((HEAD_DIM, tq), F32)) for _ in range(hp))

    if n_past:
        def past_body(j, st):
            return block(st, kp_ref, pl.multiple_of(j * tkp, tkp), tkp, vp_ref, j, None)
        states = lax.fori_loop(0, n_past, past_body, states)

    def new_body(j, st):
        return block(st, kn_ref, pl.multiple_of(j * tq, tq), tq, vn_ref, j, None)
    states = lax.fori_loop(0, i, new_body, states)

    key = lax.broadcasted_iota(jnp.int32, (tq, tq), 0)
    qry = lax.broadcasted_iota(jnp.int32, (tq, tq), 1)
    mask = (key >> mask_shift) <= (qry >> mask_shift)
    states = block(states, kn_ref, pl.multiple_of(i * tq, tq), tq, vn_ref, i, mask)
    for hh in range(hp):
        _, l_fin, acc_fin = states[hh]
        o_t = acc_fin / l_fin
        o_ref[:, hh * HEAD_DIM:(hh + 1) * HEAD_DIM] = _transpose_cols(o_t).astype(o_ref.dtype)


def _attention(q4, kn4, vn5, past, mask_shift):
    b, h, t, _ = q4.shape
    tq = _attn_tile(t)
    hp = ATTN_HEADS_PER_STEP
    assert h % hp == 0 and vn5.shape[4] == tq
    in_specs = [
        pl.BlockSpec((None, hp, tq, QK_WIDTH), lambda bb, g, i: (bb, g, i, 0)),
        pl.BlockSpec((None, hp, t, QK_WIDTH), lambda bb, g, i: (bb, g, 0, 0)),
        pl.BlockSpec((None, hp, t // tq, HEAD_DIM, tq), lambda bb, g, i: (bb, g, 0, 0, 0)),
    ]
    args = [q4, kn4, vn5]
    n_past, tkp = 0, 0
    if past is not None:
        kp4, vp5 = past
        p_len = kp4.shape[2]
        n_past, tkp = vp5.shape[2], vp5.shape[4]
        in_specs += [
            pl.BlockSpec((None, hp, p_len, QK_WIDTH), lambda bb, g, i: (bb, g, 0, 0)),
            pl.BlockSpec((None, hp, n_past, HEAD_DIM, tkp), lambda bb, g, i: (bb, g, 0, 0, 0)),
        ]
        args += [kp4, vp5]
    kern = functools.partial(_attn_kernel, tq=tq, tkp=tkp, n_past=n_past,
                             mask_shift=mask_shift, hp=hp)
    return pl.pallas_call(
        kern,
        grid=(b, h // hp, t // tq),
        in_specs=in_specs,
        out_specs=pl.BlockSpec((None, tq, hp * HEAD_DIM), lambda bb, g, i: (bb, i, g)),
        out_shape=jax.ShapeDtypeStruct((b, t, h * HEAD_DIM), BF16),
        compiler_params=_params(3, V7X_VMEM_LIMIT),
        name="attn_chunk" if mask_shift else "attn_frame",
    )(*args)


def _out_merge_kernel(oa_ref, ob_ref, ga_ref, gb_ref, x_ref, woa_ref, wob_ref, wo_ref,
                      h_ref, mg_ref, *, tc):
    d = x_ref.shape[1]
    oa = oa_ref[...]
    ob = ob_ref[...]
    for c in range(d // tc):
        sl = slice(c * tc, (c + 1) * tc)
        ta = jnp.dot(oa, woa_ref[:, sl], preferred_element_type=F32)
        tb = jnp.dot(ob, wob_ref[:, sl], preferred_element_type=F32)
        ga = jax.nn.sigmoid(ga_ref[:, sl].astype(F32))
        gb = jax.nn.sigmoid(gb_ref[:, sl].astype(F32))
        mg_ref[:, sl] = (ga * ta + gb * tb).astype(BF16)
    mg = mg_ref[...]
    for c in range(d // tc):
        sl = slice(c * tc, (c + 1) * tc)
        h_ref[:, sl] = x_ref[:, sl] + jnp.dot(mg, wo_ref[:, sl], preferred_element_type=F32)


def _out_merge(oa, ob, g_arr, x2d, woa, wob, wo):
    n, d = x2d.shape
    tm = _tile(n, 256)
    tc = _tile(d, 512)
    const = lambda i: (0, 0)
    one = pl.Buffered(1)
    return pl.pallas_call(
        functools.partial(_out_merge_kernel, tc=tc),
        grid=(n // tm,),
        in_specs=[
            pl.BlockSpec((tm, oa.shape[1]), lambda i: (i, 0)),
            pl.BlockSpec((tm, ob.shape[1]), lambda i: (i, 0)),
            pl.BlockSpec((tm, d), lambda i: (i, 0)),
            pl.BlockSpec((tm, d), lambda i: (i, 1)),
            pl.BlockSpec((tm, d), lambda i: (i, 0)),
            pl.BlockSpec(woa.shape, const, pipeline_mode=one),
            pl.BlockSpec(wob.shape, const, pipeline_mode=one),
            pl.BlockSpec(wo.shape, const, pipeline_mode=one),
        ],
        out_specs=pl.BlockSpec((tm, d), lambda i: (i, 0)),
        out_shape=jax.ShapeDtypeStruct((n, d), F32),
        scratch_shapes=[pltpu.VMEM((tm, d), BF16)],
        compiler_params=_params(1, V7X_VMEM_LIMIT),
        name="out_merge",
    )(oa, ob, g_arr, g_arr, x2d, woa, wob, wo)


def _ffn_kernel(h_ref, g_ref, wuv_ref, wug_ref, pv_ref, pg_ref, wcv_ref, wcg_ref,
                bcv_ref, bcg_ref, wd_ref, o_ref, ncv_ref, ncg_ref,
                hn_ref, acc_ref, act0_ref, act1_ref, cv_ref, cg_ref,
                *, seg, tiles_per_seq, nf):
    act_refs = (act0_ref, act1_ref)
    i = pl.program_id(0)
    j = pl.program_id(1)
    tm = h_ref.shape[0]
    nseg = tm // seg
    keep = CONV_W - 1
    seq_start = (i % tiles_per_seq) == 0

    def conv(w_ref, prev_ref, carry_ref, wc_ref, bc_ref, nc_ref):
        u = jnp.dot(hn_ref[...], w_ref[...], preferred_element_type=F32)
        wc = wc_ref[...]
        bc = bc_ref[...]
        outs = []
        for s in range(nseg):
            us = u[s * seg:(s + 1) * seg, :]
            tail = us[seg - keep:seg, :]
            if nseg == 1:
                halo = jnp.where(seq_start, prev_ref[0], carry_ref[j])
                carry_ref[j] = tail
            else:
                halo = prev_ref[s]
            nc_ref[s] = tail
            r1 = pltpu.roll(us, 1, axis=0)
            r2 = pltpu.roll(us, 2, axis=0)
            row = lax.broadcasted_iota(jnp.int32, (SUBLANES, us.shape[1]), 0)
            top1 = jnp.where(row == 0, halo[1:2, :], r1[:SUBLANES, :])
            top2 = jnp.where(row == 0, halo[0:1, :],
                             jnp.where(row == 1, halo[1:2, :], r2[:SUBLANES, :]))
            u1 = jnp.concatenate([top1, r1[SUBLANES:, :]], axis=0)
            u2 = jnp.concatenate([top2, r2[SUBLANES:, :]], axis=0)
            outs.append(bc + u2 * wc[0:1, :] + u1 * wc[1:2, :] + us * wc[2:3, :])
        return outs

    def up(slot):
        half_vals = conv(wuv_ref, pv_ref, cv_ref, wcv_ref, bcv_ref, ncv_ref)
        gates = conv(wug_ref, pg_ref, cg_ref, wcg_ref, bcg_ref, ncg_ref)
        for s in range(nseg):
            gate = gates[s]
            z = gate * (GELU_C * GELU_A * (gate * gate) + GELU_C)
            act = (gate * half_vals[s]) * (1.0 + jnp.tanh(z))
            act_refs[slot][s * seg:(s + 1) * seg, :] = act.astype(BF16)

    def down(slot):
        acc_ref[...] += jnp.dot(act_refs[slot][...], wd_ref[...], preferred_element_type=F32)

    @pl.when(j == 0)
    def _():
        hn_ref[...] = _rms(h_ref[...], g_ref[...]).astype(BF16)
        acc_ref[...] = jnp.zeros(acc_ref.shape, F32)
        up(0)

    for parity in range(2):
        @pl.when(jnp.logical_and(jnp.logical_and(j >= 1, j < nf), j % 2 == parity))
        def _():
            up(parity)
            down(1 - parity)

    @pl.when(j == nf)
    def _():
        down((nf - 1) % 2)
        o_ref[...] = h_ref[...] + acc_ref[...]


def _ffn(h2d, prev, g, w_up, w_conv, b_conv, w_down, t):
    n, d = h2d.shape
    f = w_down.shape[0]
    b = n // t
    keep = CONV_W - 1
    assert CONV_W == 3
    tm = _tile(n, 512)
    if t >= tm:
        tm = _tile(t, 512)
        seg, nb_t, tiles_per_seq = tm, 1, t // tm
        bidx = lambda i: i // tiles_per_seq
    else:
        assert tm % t == 0
        seg, nb_t, tiles_per_seq = t, tm // t, 1
        bidx = lambda i: i
    tc = _ffn_tc(f)
    nf = f // tc
    assert w_up.shape == (d, 2 * f)
    up_j = lambda j: jnp.minimum(j, nf - 1)
    down_j = lambda j: jnp.maximum(j - 1, 0)
    h2, ncv, ncg = pl.pallas_call(
        functools.partial(_ffn_kernel, seg=seg, tiles_per_seq=tiles_per_seq, nf=nf),
        grid=(n // tm, nf + 1),
        in_specs=[
            pl.BlockSpec((tm, d), lambda i, j: (i, 0)),
            pl.BlockSpec((1, d), lambda i, j: (0, 0)),
            pl.BlockSpec((d, tc), lambda i, j: (0, up_j(j))),
            pl.BlockSpec((d, tc), lambda i, j: (0, up_j(j) + nf)),
            pl.BlockSpec((nb_t, keep, tc), lambda i, j: (bidx(i), 0, up_j(j))),
            pl.BlockSpec((nb_t, keep, tc), lambda i, j: (bidx(i), 0, up_j(j) + nf)),
            pl.BlockSpec((CONV_W, tc), lambda i, j: (0, up_j(j))),
            pl.BlockSpec((CONV_W, tc), lambda i, j: (0, up_j(j) + nf)),
            pl.BlockSpec((1, tc), lambda i, j: (0, up_j(j))),
            pl.BlockSpec((1, tc), lambda i, j: (0, up_j(j) + nf)),
            pl.BlockSpec((tc, d), lambda i, j: (down_j(j), 0)),
        ],
        out_specs=[
            pl.BlockSpec((tm, d), lambda i, j: (i, 0)),
            pl.BlockSpec((nb_t, keep, tc), lambda i, j: (i, 0, up_j(j))),
            pl.BlockSpec((nb_t, keep, tc), lambda i, j: (i, 0, up_j(j))),
        ],
        out_shape=[
            jax.ShapeDtypeStruct((n, d), F32),
            jax.ShapeDtypeStruct((b * tiles_per_seq, keep, f), F32),
            jax.ShapeDtypeStruct((b * tiles_per_seq, keep, f), F32),
        ],
        scratch_shapes=[
            pltpu.VMEM((tm, d), BF16),
            pltpu.VMEM((tm, d), F32),
            pltpu.VMEM((tm, tc), BF16),
            pltpu.VMEM((tm, tc), BF16),
            pltpu.VMEM((nf, keep, tc), F32),
            pltpu.VMEM((nf, keep, tc), F32),
        ],
        compiler_params=_params(2, V7X_VMEM_LIMIT),
        name="ffn",
    )(h2d, g, w_up, w_up, prev, prev, w_conv, w_conv, b_conv, b_conv, w_down)
    last = lambda a: a.reshape(b, tiles_per_seq, keep, f)[:, tiles_per_seq - 1]
    return h2, jnp.concatenate([last(ncv), last(ncg)], axis=-1)


def _ple_final_kernel(h_ref, pe_ref, gp_ref, gf_ref, wpg_ref, wple_ref, y_ref, h3_ref, *, tc):
    d = h_ref.shape[1]
    hn = _rms(h_ref[...], gp_ref[...]).astype(BF16)
    pe = pe_ref[...].astype(BF16)
    for c in range(d // tc):
        sl = slice(c * tc, (c + 1) * tc)
        gate = jax.nn.sigmoid(jnp.dot(hn, wpg_ref[:, sl], preferred_element_type=F32))
        val = jnp.dot(pe, wple_ref[:, sl], preferred_element_type=F32)
        h3_ref[:, sl] = h_ref[:, sl] + gate * val
    y_ref[...] = _rms(h3_ref[...], gf_ref[...])


def _ple_final(h2d, pe2d, g_ple, g_final, wpg, wple):
    n, d = h2d.shape
    tm = _tile(n, 256)
    tc = _tile(d, 512)
    const = lambda i: (0, 0)
    one = pl.Buffered(1)
    return pl.pallas_call(
        functools.partial(_ple_final_kernel, tc=tc),
        grid=(n // tm,),
        in_specs=[
            pl.BlockSpec((tm, d), lambda i: (i, 0)),
            pl.BlockSpec((tm, pe2d.shape[1]), lambda i: (i, 0)),
            pl.BlockSpec((1, d), const),
            pl.BlockSpec((1, d), const),
            pl.BlockSpec(wpg.shape, const, pipeline_mode=one),
            pl.BlockSpec(wple.shape, const, pipeline_mode=one),
        ],
        out_specs=pl.BlockSpec((tm, d), lambda i: (i, 0)),
        out_shape=jax.ShapeDtypeStruct((n, d), F32),
        scratch_shapes=[pltpu.VMEM((tm, d), F32)],
        compiler_params=_params(1, V7X_VMEM_LIMIT),
        name="ple_final",
    )(h2d, pe2d, g_ple, g_final, wpg, wple)


def _cast_kernel(x_ref, o_ref):
    o_ref[...] = x_ref[...].astype(o_ref.dtype)


def _cast_rows(r, c):
    return _tile(r, max(SUBLANES, ((4 << 20) // (4 * c)) // SUBLANES * SUBLANES))


def _to_bf16(w):
    r, c = w.shape
    tr = _cast_rows(r, c)
    return pl.pallas_call(
        _cast_kernel,
        grid=(r // tr,),
        in_specs=[pl.BlockSpec((tr, c), lambda i: (i, 0))],
        out_specs=pl.BlockSpec((tr, c), lambda i: (i, 0)),
        out_shape=jax.ShapeDtypeStruct((r, c), BF16),
        compiler_params=_params(1, V7X_VMEM_LIMIT),
        name="cast_bf16",
    )(w)


def _regroup_kernel(x_ref, o_ref, *, moves):
    x = x_ref[...]
    for dst, src, width, scale in moves:
        v = x[:, src:src + width]
        if scale is not None:
            v = v * scale
        o_ref[:, dst:dst + width] = v.astype(o_ref.dtype)


def _regroup_w_in(w_in, moves, width):
    r, c = w_in.shape
    tr = _cast_rows(r, c)
    return pl.pallas_call(
        functools.partial(_regroup_kernel, moves=tuple(moves)),
        grid=(r // tr,),
        in_specs=[pl.BlockSpec((tr, c), lambda i: (i, 0))],
        out_specs=pl.BlockSpec((tr, width), lambda i: (i, 0)),
        out_shape=jax.ShapeDtypeStruct((r, width), BF16),
        compiler_params=_params(1, V7X_VMEM_LIMIT),
        name="regroup_w_in",
    )(w_in)


def _prep_weights(w, d_model):
    half = MLA_ROPE // 2
    o_cq = 0
    o_ckv = o_cq + MLA_Q_RANK
    o_kr = o_ckv + MLA_KV_RANK
    o_fq = o_kr + MLA_ROPE
    o_fk = o_fq + FOX_WIDTH
    o_fv = o_fk + FOX_WIDTH
    o_fl = o_fv + FOX_WIDTH
    o_ga = o_fl + FOX_HEADS
    o_gb = o_ga + d_model
    w_in = w["w_in"]
    d = w_in.shape[0]
    col = lambda o, n: w_in[:, o:o + n]
    moves, dst = [], 0
    for src, width, scale in [(o_cq, MLA_Q_RANK, None), (o_ckv, MLA_KV_RANK, None),
                              (o_fk, FOX_WIDTH, None), (o_fv, FOX_WIDTH, None),
                              (o_ga, d_model, None), (o_gb, d_model, None),
                              (o_fq, FOX_WIDTH, FOX_SCALE * LOG2E)]:
        moves.append((dst, src, width, scale))
        dst += width
    w_all = _regroup_w_in(w_in, moves, dst)
    zeros = lambda n: jnp.zeros((d, n), F32)
    w_s = jnp.concatenate([
        col(o_kr, MLA_ROPE), zeros(LANES - MLA_ROPE),
        col(o_kr + half, half), col(o_kr, half), zeros(LANES - MLA_ROPE),
        col(o_fl, FOX_HEADS), zeros(LANES - FOX_HEADS),
    ], axis=1).astype(BF16)

    qk = MLA_NOPE + MLA_ROPE
    wq = w["w_uq"].reshape(MLA_Q_RANK, HEADS, qk) * (MLA_SCALE * LOG2E)
    pad = jnp.zeros((MLA_Q_RANK, HEADS, LANES - MLA_ROPE), F32)
    nope = wq[:, :, :MLA_NOPE]
    x1 = wq[:, :, MLA_NOPE:MLA_NOPE + half]
    x2 = wq[:, :, MLA_NOPE + half:]
    rot = jnp.concatenate([x1, x2, pad], axis=2)
    rot_sw = jnp.concatenate([x2, x1, pad], axis=2)
    wuq3 = jnp.concatenate([nope.reshape(MLA_Q_RANK, -1), rot.reshape(MLA_Q_RANK, -1),
                            rot_sw.reshape(MLA_Q_RANK, -1)], axis=1).astype(BF16)
    wkv = w["w_ukv"].reshape(MLA_KV_RANK, HEADS, MLA_NOPE + MLA_V)
    wukv2 = jnp.concatenate([wkv[:, :, :MLA_NOPE].reshape(MLA_KV_RANK, -1),
                             wkv[:, :, MLA_NOPE:].reshape(MLA_KV_RANK, -1)], axis=1).astype(BF16)
    bf_pad = jnp.zeros((1, LANES), F32).at[0, :FOX_HEADS].set(w["b_f"])
    f = w["w_down"].shape[0]
    halve = jnp.concatenate([jnp.full((f,), 0.5, F32), jnp.ones((f,), F32)])
    return dict(
        w_all=w_all, w_s=w_s, wuq3=wuq3, wukv2=wukv2, bf_pad=bf_pad,
        g_mix=w["g_mix"][None, :], g_q=w["g_q"][None, :], g_kv=w["g_kv"][None, :],
        w_oa=_to_bf16(w["w_oa"]), w_ob=_to_bf16(w["w_ob"]), w_o=_to_bf16(w["w_o"]),
        g_ffn=w["g_ffn"][None, :], w_up=_to_bf16(w["w_up"]),
        w_conv=w["w_conv"] * halve, b_conv=(w["b_conv"] * halve)[None, :],
        w_down=_to_bf16(w["w_down"]),
        g_ple=w["g_ple"][None, :], w_pg=_to_bf16(w["w_pg"]), w_ple=_to_bf16(w["w_ple"]),
    )


def _rope_tables(pos):
    half = MLA_ROPE // 2
    inv = ROPE_THETA ** (-jnp.arange(half, dtype=F32) / half)
    ang = pos.astype(F32)[:, None] * inv[None, :]
    cos, sin = jnp.cos(ang), jnp.sin(ang)
    zero = jnp.zeros((pos.shape[0], LANES - MLA_ROPE), F32)
    return (jnp.concatenate([cos, cos, zero], axis=1),
            jnp.concatenate([-sin, sin, zero], axis=1))


def _layer(x, pe, past, pw, g_final):
    b, t, d = x.shape
    n = b * t
    p_len = 0 if past is None else past[0].shape[1]
    x2d = x.reshape(n, d)

    sm2d, a2d, fk2d, fv2d, g2d = _in_proj(x2d, pw["g_mix"], pw["w_all"], pw["w_s"], d)
    cos_t, sin_t = _rope_tables(p_len + jnp.arange(t))
    (q_mla, ckv_n, kr_n, logf, logf_pad, kc_new, v_new) = _mla_pre(
        a2d.reshape(b, t, -1), sm2d.reshape(b, t, -1), cos_t, sin_t,
        pw["g_q"], pw["g_kv"], pw["bf_pad"], pw["wuq3"], pw["wukv2"])

    fk3 = fk2d.reshape(b, t, -1)
    fv3 = fv2d.reshape(b, t, -1)
    g3 = g2d.reshape(b, t, -1)
    fq_blk = (2 * d) // FOX_WIDTH
    assert (2 * d) % FOX_WIDTH == 0
    if past is None:
        mla_past = None
        fox_past = None
        init = jnp.zeros((b, 1, LANES), F32)
    else:
        c_ckv, c_kr, c_fk, c_fv, c_logf, _ = past
        mla_past = _kv_up_call(c_ckv, c_kr, pw["wukv2"])
        lf_past = jnp.pad(c_logf, ((0, 0), (0, 0), (0, LANES - FOX_HEADS)))
        kp, vp, init = _fox_prep(None, (c_fk.reshape(b, p_len, -1), 0),
                                 (c_fv.reshape(b, p_len, -1), 0), lf_past,
                                 jnp.zeros((b, 1, LANES), F32))
        fox_past = (kp, vp)
    q_fox, k_fox, v_fox, _ = _fox_prep((g3, fq_blk), (fk3, 0), (fv3, 0), logf_pad, init)

    o_a = _attention(q_mla, kc_new, v_new, mla_past, int(math.log2(CHUNK)))
    o_b = _attention(q_fox, k_fox, v_fox, fox_past, 0)

    h2d = _out_merge(o_a.reshape(n, -1), o_b.reshape(n, -1), g2d, x2d,
                     pw["w_oa"], pw["w_ob"], pw["w_o"])

    f2 = 2 * pw["w_down"].shape[0]
    prev = (jnp.zeros((b, CONV_W - 1, f2), F32) if past is None else past[5])
    h2d, new_conv = _ffn(h2d, prev, pw["g_ffn"], pw["w_up"], pw["w_conv"], pw["b_conv"],
                         pw["w_down"], t)

    y2d = _ple_final(h2d, pe.reshape(n, -1), pw["g_ple"], g_final[None, :], pw["w_pg"], pw["w_ple"])

    fk = fk3.reshape(b, t, FOX_HEADS, FOX_HEAD_DIM)
    fv = fv3.reshape(b, t, FOX_HEADS, FOX_HEAD_DIM)
    return y2d.reshape(b, t, d), (ckv_n, kr_n, fk, fv, logf, new_conv)


def kernel(x_prompt, x_sample, cache_mla_ckv, cache_mla_krope, cache_fox_k, cache_fox_v,
           cache_fox_logf, state_ffn_conv, p_prompt, p_sample, g_mix, w_in, b_f, g_q, w_uq,
           g_kv, w_ukv, w_oa, w_ob, w_o, g_ffn, w_up, w_conv, b_conv, w_down, g_ple, w_pg,
           w_ple, g_final):
    depth = w_in.shape[0]
    assert depth == 1, "the final norm is fused into the layer's last kernel"
    d_model = x_prompt.shape[-1]
    w = {"g_mix": g_mix[0], "w_in": w_in[0], "b_f": b_f[0], "g_q": g_q[0], "w_uq": w_uq[0],
         "g_kv": g_kv[0], "w_ukv": w_ukv[0], "w_oa": w_oa[0], "w_ob": w_ob[0], "w_o": w_o[0],
         "g_ffn": g_ffn[0], "w_up": w_up[0], "w_conv": w_conv[0], "b_conv": b_conv[0],
         "w_down": w_down[0], "g_ple": g_ple[0], "w_pg": w_pg[0], "w_ple": w_ple[0]}
    pw = _prep_weights(w, d_model)

    y_p, st_p = _layer(x_prompt, p_prompt[0], None, pw, g_final)
    past = (cache_mla_ckv[0], cache_mla_krope[0], cache_fox_k[0], cache_fox_v[0],
            cache_fox_logf[0], state_ffn_conv[0])
    y_s, st_s = _layer(x_sample, p_sample[0], past, pw, g_final)

    outs = [y_p, y_s]
    for j in range(6):
        outs.append(st_p[j][None])
        outs.append(st_s[j][None])
    return tuple(outs)
```

```python
import functools
import math

import jax
import jax.numpy as jnp
from jax import lax
from jax.experimental import pallas as pl
from jax.experimental.pallas import tpu as pltpu

F32 = jnp.float32
BF16 = jnp.bfloat16

CHUNK = 64
MLA_HEADS = 8
MLA_Q_RANK = 512
MLA_KV_RANK = 512
MLA_NOPE = 128
MLA_ROPE = 64
MLA_V = 128
MLA_SCALE = (MLA_NOPE + MLA_ROPE) ** -0.5
ROPE_THETA = 10000.0
FOX_HEADS = 8
FOX_HEAD_DIM = 128
FOX_WIDTH = FOX_HEADS * FOX_HEAD_DIM
FOX_SCALE = FOX_HEAD_DIM ** -0.5
CONV_W = 3
EPS = 1e-6
NEG_INF = -1e30
LOG2E = 1.4426950408889634

HEADS = 8
HEAD_DIM = 128
QK_WIDTH = 256
LANES = 128
SUBLANES = 8
GELU_C = 0.7978845608028654
GELU_A = 0.044715
ATTN_HEADS_PER_STEP = 4
V7X_VMEM_LIMIT = 56 * 1024 * 1024

assert MLA_HEADS == HEADS and FOX_HEADS == HEADS
assert MLA_NOPE == HEAD_DIM and MLA_V == HEAD_DIM and FOX_HEAD_DIM == HEAD_DIM


def _tile(n, pref):
    if n <= pref:
        return n
    for t in range(pref, 7, -1):
        if n % t == 0 and t % 8 == 0:
            return t
    return n


def _rms(x, g):
    ms = jnp.mean(x * x, axis=-1, keepdims=True)
    return x * lax.rsqrt(ms + EPS) * g


def _split3(x):
    a1 = x.astype(BF16).astype(F32)
    r1 = x - a1
    a2 = r1.astype(BF16).astype(F32)
    a3 = (r1 - a2).astype(BF16).astype(F32)
    return a1, a2, a3


def _attn_tile(t):
    return _tile(t, 512)


def _transpose_rows(x):
    r = x.shape[0]
    if r % LANES:
        x = jnp.concatenate([x, jnp.zeros((LANES - r % LANES, x.shape[1]), x.dtype)], axis=0)
    return x.T[:, :r]


def _transpose_cols(x):
    c = x.shape[1]
    if c % LANES:
        x = jnp.concatenate([x, jnp.zeros((x.shape[0], LANES - c % LANES), x.dtype)], axis=1)
    return x.T[:c, :]


def _in_proj_tn(d_model):
    return math.gcd(1024, 2 * d_model)


def _ffn_tc(d_ff):
    return _tile(d_ff, 512)


def _params(n_axes, vmem=None, flags=None):
    return pltpu.CompilerParams(
        dimension_semantics=("arbitrary",) * n_axes,
        vmem_limit_bytes=vmem,
        flags=flags,
    )


def _in_proj_kernel(x_ref, g_ref, w_ref, ws_ref, s_ref, *rest, bounds):
    out_refs, xn_ref = rest[:-1], rest[-1]
    j = pl.program_id(1)

    @pl.when(j == 0)
    def _():
        xn = _rms(x_ref[...], g_ref[...]).astype(BF16)
        xn_ref[...] = xn
        s_ref[...] = jnp.dot(xn, ws_ref[...], preferred_element_type=F32)

    for o_ref, (lo, hi) in zip(out_refs, bounds):
        @pl.when(jnp.logical_and(j >= lo, j < hi))
        def _(o_ref=o_ref):
            o_ref[...] = jnp.dot(xn_ref[...], w_ref[...],
                                 preferred_element_type=F32).astype(o_ref.dtype)


def _in_proj(x2d, g, w_all, w_s, d_model):
    n, d = x2d.shape
    tm = _tile(n, 512)
    tn = _in_proj_tn(d_model)
    groups = [(2 * MLA_Q_RANK, F32), (FOX_WIDTH, F32), (FOX_WIDTH, F32),
              (2 * d_model + FOX_WIDTH, BF16)]
    assert w_all.shape == (d, sum(wd for wd, _ in groups))
    bounds, lo = [], 0
    for wd, _ in groups:
        bounds.append((lo, lo + wd // tn))
        lo += wd // tn
    ws = w_s.shape[1]

    def group_spec(lo, hi):
        return pl.BlockSpec((tm, tn), lambda i, j: (i, jnp.clip(j - lo, 0, hi - lo - 1)))

    return pl.pallas_call(
        functools.partial(_in_proj_kernel, bounds=tuple(bounds)),
        grid=(n // tm, lo),
        in_specs=[
            pl.BlockSpec((tm, d), lambda i, j: (i, 0)),
            pl.BlockSpec((1, d), lambda i, j: (0, 0)),
            pl.BlockSpec((d, tn), lambda i, j: (0, j)),
            pl.BlockSpec((d, ws), lambda i, j: (0, 0)),
        ],
        out_specs=[pl.BlockSpec((tm, ws), lambda i, j: (i, 0))]
        + [group_spec(lo_, hi_) for lo_, hi_ in bounds],
        out_shape=[jax.ShapeDtypeStruct((n, ws), F32)]
        + [jax.ShapeDtypeStruct((n, wd), dt) for wd, dt in groups],
        scratch_shapes=[pltpu.VMEM((tm, d), BF16)],
        compiler_params=_params(2, V7X_VMEM_LIMIT),
        name="in_proj",
    )(x2d, g, w_all, w_s)


def _kv_up(ckvn, kr128, wukv_ref, kc_ref, v_ref):
    kv = jnp.dot(ckvn.astype(BF16), wukv_ref[...], preferred_element_type=F32)
    krb = kr128.astype(BF16)
    for h in range(HEADS):
        kc_ref[h, :, 0:HEAD_DIM] = kv[:, h * HEAD_DIM:(h + 1) * HEAD_DIM].astype(BF16)
        kc_ref[h, :, HEAD_DIM:QK_WIDTH] = krb
        v_ref[h, 0] = _transpose_rows(
            kv[:, (HEADS + h) * HEAD_DIM:(HEADS + h + 1) * HEAD_DIM]).astype(BF16)


def _mla_pre_kernel(a_ref, sm_ref, c_ref, s_ref, gq_ref, gkv_ref, bf_ref, wuq_ref, wukv_ref,
                    q_ref, ckv_ref, kr_ref, lf8_ref, lfp_ref, kc_ref, v_ref):
    a = a_ref[...]
    cos = c_ref[...]
    sin = s_ref[...]
    qn = _rms(a[:, :MLA_Q_RANK], gq_ref[...]).astype(BF16)
    q3 = jnp.dot(qn, wuq_ref[...], preferred_element_type=F32)
    hw = HEADS * HEAD_DIM
    for h in range(HEADS):
        lo, hi = h * HEAD_DIM, (h + 1) * HEAD_DIM
        q_ref[h, :, 0:HEAD_DIM] = q3[:, lo:hi].astype(BF16)
        rot = q3[:, hw + lo:hw + hi] * cos + q3[:, 2 * hw + lo:2 * hw + hi] * sin
        q_ref[h, :, HEAD_DIM:QK_WIDTH] = rot.astype(BF16)

    ckvn = _rms(a[:, MLA_Q_RANK:], gkv_ref[...])
    ckv_ref[...] = ckvn
    sm = sm_ref[...]
    kr128 = sm[:, 0:LANES] * cos + sm[:, LANES:2 * LANES] * sin
    kr_ref[...] = kr128[:, :MLA_ROPE]
    z = sm[:, 2 * LANES:3 * LANES] + bf_ref[...]
    lf = jnp.minimum(z, 0.0) - jnp.log1p(jnp.exp(-jnp.abs(z)))
    lane = lax.broadcasted_iota(jnp.int32, lf.shape, 1)
    lfp_ref[...] = jnp.where(lane < FOX_HEADS, lf, 0.0)
    lf8_ref[...] = lf[:, :FOX_HEADS]
    _kv_up(ckvn, kr128, wukv_ref, kc_ref, v_ref)


def _mla_pre(a3, sm3, cos_t, sin_t, gq, gkv, bf_pad, wuq3, wukv2):
    b, t, wa = a3.shape
    tm = _attn_tile(t)
    ws = sm3.shape[2]
    const = lambda bb, i: (0, 0)
    row3 = lambda bb, i: (bb, i, 0)
    head4 = lambda bb, i: (bb, 0, i, 0)
    vt5 = lambda bb, i: (bb, 0, i, 0, 0)
    return pl.pallas_call(
        _mla_pre_kernel,
        grid=(b, t // tm),
        in_specs=[
            pl.BlockSpec((None, tm, wa), row3),
            pl.BlockSpec((None, tm, ws), row3),
            pl.BlockSpec((tm, LANES), lambda bb, i: (i, 0)),
            pl.BlockSpec((tm, LANES), lambda bb, i: (i, 0)),
            pl.BlockSpec((1, MLA_Q_RANK), const),
            pl.BlockSpec((1, MLA_KV_RANK), const),
            pl.BlockSpec((1, LANES), const),
            pl.BlockSpec(wuq3.shape, const),
            pl.BlockSpec(wukv2.shape, const),
        ],
        out_specs=[
            pl.BlockSpec((None, HEADS, tm, QK_WIDTH), head4),
            pl.BlockSpec((None, tm, MLA_KV_RANK), row3),
            pl.BlockSpec((None, tm, MLA_ROPE), row3),
            pl.BlockSpec((None, tm, FOX_HEADS), row3),
            pl.BlockSpec((None, tm, LANES), row3),
            pl.BlockSpec((None, HEADS, tm, QK_WIDTH), head4),
            pl.BlockSpec((None, HEADS, 1, HEAD_DIM, tm), vt5),
        ],
        out_shape=[
            jax.ShapeDtypeStruct((b, HEADS, t, QK_WIDTH), BF16),
            jax.ShapeDtypeStruct((b, t, MLA_KV_RANK), F32),
            jax.ShapeDtypeStruct((b, t, MLA_ROPE), F32),
            jax.ShapeDtypeStruct((b, t, FOX_HEADS), F32),
            jax.ShapeDtypeStruct((b, t, LANES), F32),
            jax.ShapeDtypeStruct((b, HEADS, t, QK_WIDTH), BF16),
            jax.ShapeDtypeStruct((b, HEADS, t // tm, HEAD_DIM, tm), BF16),
        ],
        compiler_params=_params(2, V7X_VMEM_LIMIT),
        name="mla_pre",
    )(a3, sm3, cos_t, sin_t, gq, gkv, bf_pad, wuq3, wukv2)


def _kv_up_kernel(ckv_ref, kr_ref, wukv_ref, kc_ref, v_ref):
    kr = kr_ref[...]
    kr128 = jnp.concatenate([kr, jnp.zeros((kr.shape[0], LANES - MLA_ROPE), F32)], axis=1)
    _kv_up(ckv_ref[...], kr128, wukv_ref, kc_ref, v_ref)


def _kv_up_call(ckv3, kr3, wukv2):
    b, s, _ = ckv3.shape
    ts = _attn_tile(s)
    row3 = lambda bb, i: (bb, i, 0)
    head4 = lambda bb, i: (bb, 0, i, 0)
    return pl.pallas_call(
        _kv_up_kernel,
        grid=(b, s // ts),
        in_specs=[
            pl.BlockSpec((None, ts, MLA_KV_RANK), row3),
            pl.BlockSpec((None, ts, MLA_ROPE), row3),
            pl.BlockSpec(wukv2.shape, lambda bb, i: (0, 0)),
        ],
        out_specs=[
            pl.BlockSpec((None, HEADS, ts, QK_WIDTH), head4),
            pl.BlockSpec((None, HEADS, 1, HEAD_DIM, ts), lambda bb, i: (bb, 0, i, 0, 0)),
        ],
        out_shape=[
            jax.ShapeDtypeStruct((b, HEADS, s, QK_WIDTH), BF16),
            jax.ShapeDtypeStruct((b, HEADS, s // ts, HEAD_DIM, ts), BF16),
        ],
        compiler_params=_params(2, V7X_VMEM_LIMIT),
        name="kv_up",
    )(ckv3, kr3, wukv2)


def _fox_prep_kernel(*refs, has_q):
    if has_q:
        (q_ref, k_ref, v_ref, lf_ref, init_ref,
         qo_ref, ko_ref, vo_ref, last_ref, carry_ref) = refs
    else:
        (k_ref, v_ref, lf_ref, init_ref, ko_ref, vo_ref, last_ref, carry_ref) = refs
    i = pl.program_id(1)

    @pl.when(i == 0)
    def _():
        carry_ref[...] = init_ref[...]

    lf = lf_ref[...]
    ts = lf.shape[0]
    row = lax.broadcasted_iota(jnp.int32, (ts, ts), 0)
    col = lax.broadcasted_iota(jnp.int32, (ts, ts), 1)
    tri = jnp.where(col <= row, 1.0, 0.0).astype(BF16)
    cum = carry_ref[...]
    for part in _split3(lf):
        cum = cum + jnp.dot(tri, part.astype(BF16), preferred_element_type=F32)
    carry_ref[...] = cum[ts - 1:ts, :]
    last_ref[...] = cum[ts - 1:ts, :]

    c1, c2, c3 = _split3(cum * LOG2E)
    lane = lax.broadcasted_iota(jnp.int32, (ts, LANES), 1)
    k = k_ref[...]
    v = v_ref[...]
    for h in range(HEADS):
        lo, hi = h * HEAD_DIM, (h + 1) * HEAD_DIM
        h1, h2, h3 = c1[:, h:h + 1], c2[:, h:h + 1], c3[:, h:h + 1]
        ek = jnp.where(lane < 3, 1.0,
                       jnp.where(lane == 3, -h1,
                                 jnp.where(lane == 4, -h2,
                                           jnp.where(lane == 5, -h3, 0.0))))
        ko_ref[h, :, 0:HEAD_DIM] = k[:, lo:hi].astype(BF16)
        ko_ref[h, :, HEAD_DIM:QK_WIDTH] = ek.astype(BF16)
        vo_ref[h, 0] = _transpose_rows(v[:, lo:hi]).astype(BF16)
        if has_q:
            eq = jnp.where(lane == 0, h1,
                           jnp.where(lane == 1, h2,
                                     jnp.where(lane == 2, h3,
                                               jnp.where(lane < 6, 1.0, 0.0))))
            qo_ref[h, :, 0:HEAD_DIM] = q_ref[:, lo:hi]
            qo_ref[h, :, HEAD_DIM:QK_WIDTH] = eq.astype(BF16)


def _fox_prep(q_src, k_src, v_src, lf_pad, init):
    k_arr, k_blk = k_src
    v_arr, v_blk = v_src
    b, s, _ = k_arr.shape
    ts = _attn_tile(s)
    has_q = q_src is not None
    head4 = lambda bb, i: (bb, 0, i, 0)
    in_specs, args = [], []
    if has_q:
        q_arr, q_blk = q_src
        in_specs.append(pl.BlockSpec((None, ts, FOX_WIDTH), lambda bb, i: (bb, i, q_blk)))
        args.append(q_arr)
    in_specs += [
        pl.BlockSpec((None, ts, FOX_WIDTH), lambda bb, i: (bb, i, k_blk)),
        pl.BlockSpec((None, ts, FOX_WIDTH), lambda bb, i: (bb, i, v_blk)),
        pl.BlockSpec((None, ts, LANES), lambda bb, i: (bb, i, 0)),
        pl.BlockSpec((None, 1, LANES), lambda bb, i: (bb, 0, 0)),
    ]
    args += [k_arr, v_arr, lf_pad, init]
    out_specs, out_shape = [], []
    if has_q:
        out_specs.append(pl.BlockSpec((None, HEADS, ts, QK_WIDTH), head4))
        out_shape.append(jax.ShapeDtypeStruct((b, HEADS, s, QK_WIDTH), BF16))
    out_specs += [
        pl.BlockSpec((None, HEADS, ts, QK_WIDTH), head4),
        pl.BlockSpec((None, HEADS, 1, HEAD_DIM, ts), lambda bb, i: (bb, 0, i, 0, 0)),
        pl.BlockSpec((None, 1, LANES), lambda bb, i: (bb, 0, 0)),
    ]
    out_shape += [
        jax.ShapeDtypeStruct((b, HEADS, s, QK_WIDTH), BF16),
        jax.ShapeDtypeStruct((b, HEADS, s // ts, HEAD_DIM, ts), BF16),
        jax.ShapeDtypeStruct((b, 1, LANES), F32),
    ]
    return pl.pallas_call(
        functools.partial(_fox_prep_kernel, has_q=has_q),
        grid=(b, s // ts),
        in_specs=in_specs,
        out_specs=out_specs,
        out_shape=out_shape,
        scratch_shapes=[pltpu.VMEM((1, LANES), F32)],
        compiler_params=_params(2, V7X_VMEM_LIMIT),
        name="fox_prep_q" if has_q else "fox_prep_past",
    )(*args)


def _attn_kernel(*refs, tq, tkp, n_past, mask_shift, hp):
    n_in = 5 if n_past else 3
    q_ref, kn_ref, vn_ref = refs[:3]
    kp_ref, vp_ref = refs[3:5] if n_past else (None, None)
    o_ref = refs[n_in]
    acc_refs = refs[n_in + 1:]
    i = pl.program_id(2)
    for acc_ref in acc_refs:
        acc_ref[...] = jnp.zeros(acc_ref.shape, F32)

    def block(states, k_ref, off, tk, vt_ref, j, mask):
        scores = [lax.dot_general(k_ref[hh, pl.ds(off, tk), :], q_ref[hh],
                                  (((1,), (1,)), ((), ())), preferred_element_type=F32)
                  for hh in range(hp)]
        out = []
        for hh in range(hp):
            m_prev, l_prev = states[hh]
            s_t = scores[hh]
            if mask is not None:
                s_t = jnp.where(mask, s_t, NEG_INF)
            m_new = jnp.maximum(m_prev, jnp.max(s_t, axis=0, keepdims=True))
            alpha = jnp.exp2(m_prev - m_new)
            p_t = jnp.exp2(s_t - m_new)
            l_new = alpha * l_prev + jnp.sum(p_t, axis=0, keepdims=True)
            acc_refs[hh][...] = alpha * acc_refs[hh][...] + jnp.dot(
                vt_ref[hh, j], p_t.astype(BF16), preferred_element_type=F32)
            out.append((m_new, l_new))
        return tuple(out)

    states = tuple((jnp.full((1, tq), NEG_INF, F32), jnp.zeros((1, tq), F32))
                   for _ in range(hp))

    if n_past:
        def past_body(j, st):
            return block(st, kp_ref, pl.multiple_of(j * tkp, tkp), tkp, vp_ref, j, None)
        states = lax.fori_loop(0, n_past, past_body, states)

    def new_body(j, st):
        return block(st, kn_ref, pl.multiple_of(j * tq, tq), tq, vn_ref, j, None)
    states = lax.fori_loop(0, i, new_body, states)

    key = lax.broadcasted_iota(jnp.int32, (tq, tq), 0)
    qry = lax.broadcasted_iota(jnp.int32, (tq, tq), 1)
    mask = (key >> mask_shift) <= (qry >> mask_shift)
    states = block(states, kn_ref, pl.multiple_of(i * tq, tq), tq, vn_ref, i, mask)
    for hh in range(hp):
        o_t = acc_refs[hh][...] / states[hh][1]
        o_ref[:, hh * HEAD_DIM:(hh + 1) * HEAD_DIM] = _transpose_cols(o_t).astype(o_ref.dtype)


def _attention(q4, kn4, vn5, past, mask_shift):
    b, h, t, _ = q4.shape
    tq = _attn_tile(t)
    hp = ATTN_HEADS_PER_STEP
    assert h % hp == 0 and vn5.shape[4] == tq
    in_specs = [
        pl.BlockSpec((None, hp, tq, QK_WIDTH), lambda bb, g, i: (bb, g, i, 0)),
        pl.BlockSpec((None, hp, t, QK_WIDTH), lambda bb, g, i: (bb, g, 0, 0)),
        pl.BlockSpec((None, hp, t // tq, HEAD_DIM, tq), lambda bb, g, i: (bb, g, 0, 0, 0)),
    ]
    args = [q4, kn4, vn5]
    n_past, tkp = 0, 0
    if past is not None:
        kp4, vp5 = past
        p_len = kp4.shape[2]
        n_past, tkp = vp5.shape[2], vp5.shape[4]
        in_specs += [
            pl.BlockSpec((None, hp, p_len, QK_WIDTH), lambda bb, g, i: (bb, g, 0, 0)),
            pl.BlockSpec((None, hp, n_past, HEAD_DIM, tkp), lambda bb, g, i: (bb, g, 0, 0, 0)),
        ]
        args += [kp4, vp5]
    kern = functools.partial(_attn_kernel, tq=tq, tkp=tkp, n_past=n_past,
                             mask_shift=mask_shift, hp=hp)
    return pl.pallas_call(
        kern,
        grid=(b, h // hp, t // tq),
        in_specs=in_specs,
        out_specs=pl.BlockSpec((None, tq, hp * HEAD_DIM), lambda bb, g, i: (bb, i, g)),
        out_shape=jax.ShapeDtypeStruct((b, t, h * HEAD_DIM), BF16),
        scratch_shapes=[pltpu.VMEM((HEAD_DIM, tq), F32) for _ in range(hp)],
        compiler_params=_params(3, V7X_VMEM_LIMIT),
        name="attn_chunk" if mask_shift else "attn_frame",
    )(*args)


def _out_merge_kernel(oa_ref, ob_ref, ga_ref, gb_ref, x_ref, woa_ref, wob_ref, wo_ref,
                      h_ref, mg_ref, *, tc):
    d = x_ref.shape[1]
    oa = oa_ref[...]
    ob = ob_ref[...]
    for c in range(d // tc):
        sl = slice(c * tc, (c + 1) * tc)
        ta = jnp.dot(oa, woa_ref[:, sl], preferred_element_type=F32)
        tb = jnp.dot(ob, wob_ref[:, sl], preferred_element_type=F32)
        ga = jax.nn.sigmoid(ga_ref[:, sl].astype(F32))
        gb = jax.nn.sigmoid(gb_ref[:, sl].astype(F32))
        mg_ref[:, sl] = (ga * ta + gb * tb).astype(BF16)
    mg = mg_ref[...]
    for c in range(d // tc):
        sl = slice(c * tc, (c + 1) * tc)
        h_ref[:, sl] = x_ref[:, sl] + jnp.dot(mg, wo_ref[:, sl], preferred_element_type=F32)


def _out_merge(oa, ob, g_arr, x2d, woa, wob, wo):
    n, d = x2d.shape
    tm = _tile(n, 256)
    tc = _tile(d, 512)
    const = lambda i: (0, 0)
    one = pl.Buffered(1)
    return pl.pallas_call(
        functools.partial(_out_merge_kernel, tc=tc),
        grid=(n // tm,),
        in_specs=[
            pl.BlockSpec((tm, oa.shape[1]), lambda i: (i, 0)),
            pl.BlockSpec((tm, ob.shape[1]), lambda i: (i, 0)),
            pl.BlockSpec((tm, d), lambda i: (i, 0)),
            pl.BlockSpec((tm, d), lambda i: (i, 1)),
            pl.BlockSpec((tm, d), lambda i: (i, 0)),
            pl.BlockSpec(woa.shape, const, pipeline_mode=one),
            pl.BlockSpec(wob.shape, const, pipeline_mode=one),
            pl.BlockSpec(wo.shape, const, pipeline_mode=one),
        ],
        out_specs=pl.BlockSpec((tm, d), lambda i: (i, 0)),
        out_shape=jax.ShapeDtypeStruct((n, d), F32),
        scratch_shapes=[pltpu.VMEM((tm, d), BF16)],
        compiler_params=_params(1, V7X_VMEM_LIMIT),
        name="out_merge",
    )(oa, ob, g_arr, g_arr, x2d, woa, wob, wo)


def _ffn_kernel(h_ref, g_ref, wuv_ref, wug_ref, pv_ref, pg_ref, wcv_ref, wcg_ref,
                bcv_ref, bcg_ref, wd_ref, o_ref, ncv_ref, ncg_ref,
                hn_ref, acc_ref, act0_ref, act1_ref, cv_ref, cg_ref,
                *, seg, tiles_per_seq, nf):
    act_refs = (act0_ref, act1_ref)
    i = pl.program_id(0)
    j = pl.program_id(1)
    tm = h_ref.shape[0]
    nseg = tm // seg
    keep = CONV_W - 1
    seq_start = (i % tiles_per_seq) == 0

    def conv(w_ref, prev_ref, carry_ref, wc_ref, bc_ref, nc_ref):
        u = jnp.dot(hn_ref[...], w_ref[...], preferred_element_type=F32)
        wc = wc_ref[...]
        bc = bc_ref[...]
        outs = []
        for s in range(nseg):
            us = u[s * seg:(s + 1) * seg, :]
            tail = us[seg - keep:seg, :]
            if nseg == 1:
                halo = jnp.where(seq_start, prev_ref[0], carry_ref[j])
                carry_ref[j] = tail
            else:
                halo = prev_ref[s]
            nc_ref[s] = tail
            r1 = pltpu.roll(us, 1, axis=0)
            r2 = pltpu.roll(us, 2, axis=0)
            row = lax.broadcasted_iota(jnp.int32, (SUBLANES, us.shape[1]), 0)
            top1 = jnp.where(row == 0, halo[1:2, :], r1[:SUBLANES, :])
            top2 = jnp.where(row == 0, halo[0:1, :],
                             jnp.where(row == 1, halo[1:2, :], r2[:SUBLANES, :]))
            u1 = jnp.concatenate([top1, r1[SUBLANES:, :]], axis=0)
            u2 = jnp.concatenate([top2, r2[SUBLANES:, :]], axis=0)
            outs.append(bc + u2 * wc[0:1, :] + u1 * wc[1:2, :] + us * wc[2:3, :])
        return outs

    def up(slot):
        half_vals = conv(wuv_ref, pv_ref, cv_ref, wcv_ref, bcv_ref, ncv_ref)
        gates = conv(wug_ref, pg_ref, cg_ref, wcg_ref, bcg_ref, ncg_ref)
        for s in range(nseg):
            gate = gates[s]
            z = gate * (GELU_C * GELU_A * (gate * gate) + GELU_C)
            act = (gate * half_vals[s]) * (1.0 + jnp.tanh(z))
            act_refs[slot][s * seg:(s + 1) * seg, :] = act.astype(BF16)

    def down(slot):
        acc_ref[...] += jnp.dot(act_refs[slot][...], wd_ref[...], preferred_element_type=F32)

    @pl.when(j == 0)
    def _():
        hn_ref[...] = _rms(h_ref[...], g_ref[...]).astype(BF16)
        acc_ref[...] = jnp.zeros(acc_ref.shape, F32)
        up(0)

    for parity in range(2):
        @pl.when(jnp.logical_and(jnp.logical_and(j >= 1, j < nf), j % 2 == parity))
        def _():
            up(parity)
            down(1 - parity)

    @pl.when(j == nf)
    def _():
        down((nf - 1) % 2)
        o_ref[...] = h_ref[...] + acc_ref[...]


def _ffn(h2d, prev, g, w_up, w_conv, b_conv, w_down, t):
    n, d = h2d.shape
    f = w_down.shape[0]
    b = n // t
    keep = CONV_W - 1
    assert CONV_W == 3
    tm = _tile(n, 512)
    if t >= tm:
        tm = _tile(t, 512)
        seg, nb_t, tiles_per_seq = tm, 1, t // tm
        bidx = lambda i: i // tiles_per_seq
    else:
        assert tm % t == 0
        seg, nb_t, tiles_per_seq = t, tm // t, 1
        bidx = lambda i: i
    tc = _ffn_tc(f)
    nf = f // tc
    assert w_up.shape == (d, 2 * f)
    up_j = lambda j: jnp.minimum(j, nf - 1)
    down_j = lambda j: jnp.maximum(j - 1, 0)
    h2, ncv, ncg = pl.pallas_call(
        functools.partial(_ffn_kernel, seg=seg, tiles_per_seq=tiles_per_seq, nf=nf),
        grid=(n // tm, nf + 1),
        in_specs=[
            pl.BlockSpec((tm, d), lambda i, j: (i, 0)),
            pl.BlockSpec((1, d), lambda i, j: (0, 0)),
            pl.BlockSpec((d, tc), lambda i, j: (0, up_j(j))),
            pl.BlockSpec((d, tc), lambda i, j: (0, up_j(j) + nf)),
            pl.BlockSpec((nb_t, keep, tc), lambda i, j: (bidx(i), 0, up_j(j))),
            pl.BlockSpec((nb_t, keep, tc), lambda i, j: (bidx(i), 0, up_j(j) + nf)),
            pl.BlockSpec((CONV_W, tc), lambda i, j: (0, up_j(j))),
            pl.BlockSpec((CONV_W, tc), lambda i, j: (0, up_j(j) + nf)),
            pl.BlockSpec((1, tc), lambda i, j: (0, up_j(j))),
            pl.BlockSpec((1, tc), lambda i, j: (0, up_j(j) + nf)),
            pl.BlockSpec((tc, d), lambda i, j: (down_j(j), 0)),
        ],
        out_specs=[
            pl.BlockSpec((tm, d), lambda i, j: (i, 0)),
            pl.BlockSpec((nb_t, keep, tc), lambda i, j: (i, 0, up_j(j))),
            pl.BlockSpec((nb_t, keep, tc), lambda i, j: (i, 0, up_j(j))),
        ],
        out_shape=[
            jax.ShapeDtypeStruct((n, d), F32),
            jax.ShapeDtypeStruct((b * tiles_per_seq, keep, f), F32),
            jax.ShapeDtypeStruct((b * tiles_per_seq, keep, f), F32),
        ],
        scratch_shapes=[
            pltpu.VMEM((tm, d), BF16),
            pltpu.VMEM((tm, d), F32),
            pltpu.VMEM((tm, tc), BF16),
            pltpu.VMEM((tm, tc), BF16),
            pltpu.VMEM((nf, keep, tc), F32),
            pltpu.VMEM((nf, keep, tc), F32),
        ],
        compiler_params=_params(2, V7X_VMEM_LIMIT),
        name="ffn",
    )(h2d, g, w_up, w_up, prev, prev, w_conv, w_conv, b_conv, b_conv, w_down)
    last = lambda a: a.reshape(b, tiles_per_seq, keep, f)[:, tiles_per_seq - 1]
    return h2, jnp.concatenate([last(ncv), last(ncg)], axis=-1)


def _ple_final_kernel(h_ref, pe_ref, gp_ref, gf_ref, wpg_ref, wple_ref, y_ref, h3_ref, *, tc):
    d = h_ref.shape[1]
    hn = _rms(h_ref[...], gp_ref[...]).astype(BF16)
    pe = pe_ref[...].astype(BF16)
    for c in range(d // tc):
        sl = slice(c * tc, (c + 1) * tc)
        gate = jax.nn.sigmoid(jnp.dot(hn, wpg_ref[:, sl], preferred_element_type=F32))
        val = jnp.dot(pe, wple_ref[:, sl], preferred_element_type=F32)
        h3_ref[:, sl] = h_ref[:, sl] + gate * val
    y_ref[...] = _rms(h3_ref[...], gf_ref[...])


def _ple_final(h2d, pe2d, g_ple, g_final, wpg, wple):
    n, d = h2d.shape
    tm = _tile(n, 256)
    tc = _tile(d, 512)
    const = lambda i: (0, 0)
    one = pl.Buffered(1)
    return pl.pallas_call(
        functools.partial(_ple_final_kernel, tc=tc),
        grid=(n // tm,),
        in_specs=[
            pl.BlockSpec((tm, d), lambda i: (i, 0)),
            pl.BlockSpec((tm, pe2d.shape[1]), lambda i: (i, 0)),
            pl.BlockSpec((1, d), const),
            pl.BlockSpec((1, d), const),
            pl.BlockSpec(wpg.shape, const, pipeline_mode=one),
            pl.BlockSpec(wple.shape, const, pipeline_mode=one),
        ],
        out_specs=pl.BlockSpec((tm, d), lambda i: (i, 0)),
        out_shape=jax.ShapeDtypeStruct((n, d), F32),
        scratch_shapes=[pltpu.VMEM((tm, d), F32)],
        compiler_params=_params(1, V7X_VMEM_LIMIT),
        name="ple_final",
    )(h2d, pe2d, g_ple, g_final, wpg, wple)


def _cast_kernel(x_ref, o_ref):
    o_ref[...] = x_ref[...].astype(o_ref.dtype)


def _cast_rows(r, c):
    return _tile(r, max(SUBLANES, ((8 << 20) // (4 * c)) // SUBLANES * SUBLANES))


def _to_bf16(w):
    r, c = w.shape
    tr = _cast_rows(r, c)
    return pl.pallas_call(
        _cast_kernel,
        grid=(r // tr,),
        in_specs=[pl.BlockSpec((tr, c), lambda i: (i, 0))],
        out_specs=pl.BlockSpec((tr, c), lambda i: (i, 0)),
        out_shape=jax.ShapeDtypeStruct((r, c), BF16),
        compiler_params=_params(1, V7X_VMEM_LIMIT),
        name="cast_bf16",
    )(w)


def _regroup_kernel(x_ref, o_ref, *, moves):
    x = x_ref[...]
    for dst, src, width, scale in moves:
        v = x[:, src:src + width]
        if scale is not None:
            v = v * scale
        o_ref[:, dst:dst + width] = v.astype(o_ref.dtype)


def _regroup_w_in(w_in, moves, width):
    r, c = w_in.shape
    tr = _cast_rows(r, c)
    return pl.pallas_call(
        functools.partial(_regroup_kernel, moves=tuple(moves)),
        grid=(r // tr,),
        in_specs=[pl.BlockSpec((tr, c), lambda i: (i, 0))],
        out_specs=pl.BlockSpec((tr, width), lambda i: (i, 0)),
        out_shape=jax.ShapeDtypeStruct((r, width), BF16),
        compiler_params=_params(1, V7X_VMEM_LIMIT),
        name="regroup_w_in",
    )(w_in)


def _prep_weights(w, d_model):
    half = MLA_ROPE // 2
    o_cq = 0
    o_ckv = o_cq + MLA_Q_RANK
    o_kr = o_ckv + MLA_KV_RANK
    o_fq = o_kr + MLA_ROPE
    o_fk = o_fq + FOX_WIDTH
    o_fv = o_fk + FOX_WIDTH
    o_fl = o_fv + FOX_WIDTH
    o_ga = o_fl + FOX_HEADS
    o_gb = o_ga + d_model
    w_in = w["w_in"]
    d = w_in.shape[0]
    col = lambda o, n: w_in[:, o:o + n]
    moves, dst = [], 0
    for src, width, scale in [(o_cq, MLA_Q_RANK, None), (o_ckv, MLA_KV_RANK, None),
                              (o_fk, FOX_WIDTH, None), (o_fv, FOX_WIDTH, None),
                              (o_ga, d_model, None), (o_gb, d_model, None),
                              (o_fq, FOX_WIDTH, FOX_SCALE * LOG2E)]:
        moves.append((dst, src, width, scale))
        dst += width
    w_all = _regroup_w_in(w_in, moves, dst)
    zeros = lambda n: jnp.zeros((d, n), F32)
    w_s = jnp.concatenate([
        col(o_kr, MLA_ROPE), zeros(LANES - MLA_ROPE),
        col(o_kr + half, half), col(o_kr, half), zeros(LANES - MLA_ROPE),
        col(o_fl, FOX_HEADS), zeros(LANES - FOX_HEADS),
    ], axis=1).astype(BF16)

    qk = MLA_NOPE + MLA_ROPE
    wq = w["w_uq"].reshape(MLA_Q_RANK, HEADS, qk) * (MLA_SCALE * LOG2E)
    pad = jnp.zeros((MLA_Q_RANK, HEADS, LANES - MLA_ROPE), F32)
    nope = wq[:, :, :MLA_NOPE]
    x1 = wq[:, :, MLA_NOPE:MLA_NOPE + half]
    x2 = wq[:, :, MLA_NOPE + half:]
    rot = jnp.concatenate([x1, x2, pad], axis=2)
    rot_sw = jnp.concatenate([x2, x1, pad], axis=2)
    wuq3 = jnp.concatenate([nope.reshape(MLA_Q_RANK, -1), rot.reshape(MLA_Q_RANK, -1),
                            rot_sw.reshape(MLA_Q_RANK, -1)], axis=1).astype(BF16)
    wkv = w["w_ukv"].reshape(MLA_KV_RANK, HEADS, MLA_NOPE + MLA_V)
    wukv2 = jnp.concatenate([wkv[:, :, :MLA_NOPE].reshape(MLA_KV_RANK, -1),
                             wkv[:, :, MLA_NOPE:].reshape(MLA_KV_RANK, -1)], axis=1).astype(BF16)
    bf_pad = jnp.zeros((1, LANES), F32).at[0, :FOX_HEADS].set(w["b_f"])
    f = w["w_down"].shape[0]
    halve = jnp.concatenate([jnp.full((f,), 0.5, F32), jnp.ones((f,), F32)])
    return dict(
        w_all=w_all, w_s=w_s, wuq3=wuq3, wukv2=wukv2, bf_pad=bf_pad,
        g_mix=w["g_mix"][None, :], g_q=w["g_q"][None, :], g_kv=w["g_kv"][None, :],
        w_oa=_to_bf16(w["w_oa"]), w_ob=_to_bf16(w["w_ob"]), w_o=_to_bf16(w["w_o"]),
        g_ffn=w["g_ffn"][None, :], w_up=_to_bf16(w["w_up"]),
        w_conv=w["w_conv"] * halve, b_conv=(w["b_conv"] * halve)[None, :],
        w_down=_to_bf16(w["w_down"]),
        g_ple=w["g_ple"][None, :], w_pg=_to_bf16(w["w_pg"]), w_ple=_to_bf16(w["w_ple"]),
    )


def _rope_tables(pos):
    half = MLA_ROPE // 2
    inv = ROPE_THETA ** (-jnp.arange(half, dtype=F32) / half)
    ang = pos.astype(F32)[:, None] * inv[None, :]
    cos, sin = jnp.cos(ang), jnp.sin(ang)
    zero = jnp.zeros((pos.shape[0], LANES - MLA_ROPE), F32)
    return (jnp.concatenate([cos, cos, zero], axis=1),
            jnp.concatenate([-sin, sin, zero], axis=1))


def _layer(x, pe, past, pw, g_final):
    b, t, d = x.shape
    n = b * t
    p_len = 0 if past is None else past[0].shape[1]
    x2d = x.reshape(n, d)

    sm2d, a2d, fk2d, fv2d, g2d = _in_proj(x2d, pw["g_mix"], pw["w_all"], pw["w_s"], d)
    cos_t, sin_t = _rope_tables(p_len + jnp.arange(t))
    (q_mla, ckv_n, kr_n, logf, logf_pad, kc_new, v_new) = _mla_pre(
        a2d.reshape(b, t, -1), sm2d.reshape(b, t, -1), cos_t, sin_t,
        pw["g_q"], pw["g_kv"], pw["bf_pad"], pw["wuq3"], pw["wukv2"])

    fk3 = fk2d.reshape(b, t, -1)
    fv3 = fv2d.reshape(b, t, -1)
    g3 = g2d.reshape(b, t, -1)
    fq_blk = (2 * d) // FOX_WIDTH
    assert (2 * d) % FOX_WIDTH == 0
    if past is None:
        mla_past = None
        fox_past = None
        init = jnp.zeros((b, 1, LANES), F32)
    else:
        c_ckv, c_kr, c_fk, c_fv, c_logf, _ = past
        mla_past = _kv_up_call(c_ckv, c_kr, pw["wukv2"])
        lf_past = jnp.pad(c_logf, ((0, 0), (0, 0), (0, LANES - FOX_HEADS)))
        kp, vp, init = _fox_prep(None, (c_fk.reshape(b, p_len, -1), 0),
                                 (c_fv.reshape(b, p_len, -1), 0), lf_past,
                                 jnp.zeros((b, 1, LANES), F32))
        fox_past = (kp, vp)
    q_fox, k_fox, v_fox, _ = _fox_prep((g3, fq_blk), (fk3, 0), (fv3, 0), logf_pad, init)

    o_a = _attention(q_mla, kc_new, v_new, mla_past, int(math.log2(CHUNK)))
    o_b = _attention(q_fox, k_fox, v_fox, fox_past, 0)

    h2d = _out_merge(o_a.reshape(n, -1), o_b.reshape(n, -1), g2d, x2d,
                     pw["w_oa"], pw["w_ob"], pw["w_o"])

    f2 = 2 * pw["w_down"].shape[0]
    prev = (jnp.zeros((b, CONV_W - 1, f2), F32) if past is None else past[5])
    h2d, new_conv = _ffn(h2d, prev, pw["g_ffn"], pw["w_up"], pw["w_conv"], pw["b_conv"],
                         pw["w_down"], t)

    y2d = _ple_final(h2d, pe.reshape(n, -1), pw["g_ple"], g_final[None, :], pw["w_pg"], pw["w_ple"])

    fk = fk3.reshape(b, t, FOX_HEADS, FOX_HEAD_DIM)
    fv = fv3.reshape(b, t, FOX_HEADS, FOX_HEAD_DIM)
    return y2d.reshape(b, t, d), (ckv_n, kr_n, fk, fv, logf, new_conv)


def kernel(x_prompt, x_sample, cache_mla_ckv, cache_mla_krope, cache_fox_k, cache_fox_v,
           cache_fox_logf, state_ffn_conv, p_prompt, p_sample, g_mix, w_in, b_f, g_q, w_uq,
           g_kv, w_ukv, w_oa, w_ob, w_o, g_ffn, w_up, w_conv, b_conv, w_down, g_ple, w_pg,
           w_ple, g_final):
    depth = w_in.shape[0]
    assert depth == 1, "the final norm is fused into the layer's last kernel"
    d_model = x_prompt.shape[-1]
    layer0 = lambda a: a.reshape(a.shape[1:])
    stacked = lambda a: a.reshape((1,) + a.shape)
    w = {"g_mix": g_mix, "w_in": w_in, "b_f": b_f, "g_q": g_q, "w_uq": w_uq,
         "g_kv": g_kv, "w_ukv": w_ukv, "w_oa": w_oa, "w_ob": w_ob, "w_o": w_o,
         "g_ffn": g_ffn, "w_up": w_up, "w_conv": w_conv, "b_conv": b_conv,
         "w_down": w_down, "g_ple": g_ple, "w_pg": w_pg, "w_ple": w_ple}
    pw = _prep_weights({k: layer0(v) for k, v in w.items()}, d_model)

    y_p, st_p = _layer(x_prompt, layer0(p_prompt), None, pw, g_final)
    past = tuple(layer0(a) for a in (cache_mla_ckv, cache_mla_krope, cache_fox_k, cache_fox_v,
                                     cache_fox_logf, state_ffn_conv))
    y_s, st_s = _layer(x_sample, layer0(p_sample), past, pw, g_final)

    outs = [y_p, y_s]
    for j in range(6):
        outs.append(stacked(st_p[j]))
        outs.append(stacked(st_s[j]))
    return tuple(outs)
```

```python
import functools
import math

import jax
import jax.numpy as jnp
from jax import lax
from jax.experimental import pallas as pl
from jax.experimental.pallas import tpu as pltpu

F32 = jnp.float32
BF16 = jnp.bfloat16

CHUNK = 64
MLA_HEADS = 8
MLA_Q_RANK = 512
MLA_KV_RANK = 512
MLA_NOPE = 128
MLA_ROPE = 64
MLA_V = 128
MLA_SCALE = (MLA_NOPE + MLA_ROPE) ** -0.5
ROPE_THETA = 10000.0
FOX_HEADS = 8
FOX_HEAD_DIM = 128
FOX_WIDTH = FOX_HEADS * FOX_HEAD_DIM
FOX_SCALE = FOX_HEAD_DIM ** -0.5
CONV_W = 3
EPS = 1e-6
NEG_INF = -1e30
LOG2E = 1.4426950408889634

HEADS = 8
HEAD_DIM = 128
QK_WIDTH = 256
LANES = 128
SUBLANES = 8
GELU_C = 0.7978845608028654
GELU_A = 0.044715
ATTN_HEADS_PER_STEP = 4
V7X_VMEM_LIMIT = 56 * 1024 * 1024

assert MLA_HEADS == HEADS and FOX_HEADS == HEADS
assert MLA_NOPE == HEAD_DIM and MLA_V == HEAD_DIM and FOX_HEAD_DIM == HEAD_DIM


def _tile(n, pref):
    if n <= pref:
        return n
    for t in range(pref, 7, -1):
        if n % t == 0 and t % 8 == 0:
            return t
    return n


def _rms(x, g):
    ms = jnp.mean(x * x, axis=-1, keepdims=True)
    return x * lax.rsqrt(ms + EPS) * g


def _split3(x):
    a1 = x.astype(BF16).astype(F32)
    r1 = x - a1
    a2 = r1.astype(BF16).astype(F32)
    a3 = (r1 - a2).astype(BF16).astype(F32)
    return a1, a2, a3


def _attn_tile(t):
    return _tile(t, 512)


def _transpose_rows(x):
    r = x.shape[0]
    if r % LANES:
        x = jnp.concatenate([x, jnp.zeros((LANES - r % LANES, x.shape[1]), x.dtype)], axis=0)
    return x.T[:, :r]


def _transpose_cols(x):
    c = x.shape[1]
    if c % LANES:
        x = jnp.concatenate([x, jnp.zeros((x.shape[0], LANES - c % LANES), x.dtype)], axis=1)
    return x.T[:c, :]


def _in_proj_tn(d_model):
    return math.gcd(1024, 2 * d_model)


def _ffn_tc(d_ff):
    return _tile(d_ff, 512)


def _params(n_axes, vmem=None, flags=None):
    return pltpu.CompilerParams(
        dimension_semantics=("arbitrary",) * n_axes,
        vmem_limit_bytes=vmem,
        flags=flags,
    )


def _in_proj_kernel(x_ref, g_ref, w_ref, ws_ref, s_ref, *rest, bounds):
    out_refs, xn_ref = rest[:-1], rest[-1]
    j = pl.program_id(1)

    @pl.when(j == 0)
    def _():
        xn = _rms(x_ref[...], g_ref[...]).astype(BF16)
        xn_ref[...] = xn
        s_ref[...] = jnp.dot(xn, ws_ref[...], preferred_element_type=F32)

    for o_ref, (lo, hi) in zip(out_refs, bounds):
        @pl.when(jnp.logical_and(j >= lo, j < hi))
        def _(o_ref=o_ref):
            z = jnp.dot(xn_ref[...], w_ref[...], preferred_element_type=F32)
            if len(o_ref.shape) == 3:
                heads = [z[:, h * HEAD_DIM:(h + 1) * HEAD_DIM] for h in range(o_ref.shape[1])]
                o_ref[...] = jnp.swapaxes(jnp.stack(heads, axis=0), 0, 1)
            else:
                o_ref[...] = z.astype(o_ref.dtype)


def _in_proj(x2d, g, w_all, w_s, d_model):
    n, d = x2d.shape
    tm = _tile(n, 512)
    tn = _in_proj_tn(d_model)
    assert tn == FOX_WIDTH, "the fk / fv groups are written one whole (heads, 128) row per step"
    groups = [(2 * MLA_Q_RANK, F32, False), (FOX_WIDTH, F32, True), (FOX_WIDTH, F32, True),
              (2 * d_model + FOX_WIDTH, BF16, False)]
    assert w_all.shape == (d, sum(wd for wd, _, _ in groups))
    bounds, lo = [], 0
    for wd, _, _ in groups:
        bounds.append((lo, lo + wd // tn))
        lo += wd // tn
    ws = w_s.shape[1]
    out_specs = [pl.BlockSpec((tm, ws), lambda i, j: (i, 0))]
    out_shape = [jax.ShapeDtypeStruct((n, ws), F32)]
    for (wd, dt, per_head), (lo_, hi_) in zip(groups, bounds):
        if per_head:
            out_specs.append(pl.BlockSpec((tm, FOX_HEADS, HEAD_DIM), lambda i, j: (i, 0, 0)))
            out_shape.append(jax.ShapeDtypeStruct((n, FOX_HEADS, HEAD_DIM), dt))
        else:
            out_specs.append(pl.BlockSpec(
                (tm, tn), lambda i, j, lo_=lo_, hi_=hi_: (i, jnp.clip(j - lo_, 0, hi_ - lo_ - 1))))
            out_shape.append(jax.ShapeDtypeStruct((n, wd), dt))

    return pl.pallas_call(
        functools.partial(_in_proj_kernel, bounds=tuple(bounds)),
        grid=(n // tm, lo),
        in_specs=[
            pl.BlockSpec((tm, d), lambda i, j: (i, 0)),
            pl.BlockSpec((1, d), lambda i, j: (0, 0)),
            pl.BlockSpec((d, tn), lambda i, j: (0, j)),
            pl.BlockSpec((d, ws), lambda i, j: (0, 0)),
        ],
        out_specs=out_specs,
        out_shape=out_shape,
        scratch_shapes=[pltpu.VMEM((tm, d), BF16)],
        compiler_params=_params(2, V7X_VMEM_LIMIT),
        name="in_proj",
    )(x2d, g, w_all, w_s)


def _kv_up(ckvn, kr128, wukv_ref, kc_ref, v_ref):
    kv = jnp.dot(ckvn.astype(BF16), wukv_ref[...], preferred_element_type=F32)
    krb = kr128.astype(BF16)
    for h in range(HEADS):
        kc_ref[h, :, 0:HEAD_DIM] = kv[:, h * HEAD_DIM:(h + 1) * HEAD_DIM].astype(BF16)
        kc_ref[h, :, HEAD_DIM:QK_WIDTH] = krb
        v_ref[h, 0] = _transpose_rows(
            kv[:, (HEADS + h) * HEAD_DIM:(HEADS + h + 1) * HEAD_DIM]).astype(BF16)


def _mla_pre_kernel(a_ref, sm_ref, c_ref, s_ref, gq_ref, gkv_ref, bf_ref, wuq_ref, wukv_ref,
                    q_ref, ckv_ref, kr_ref, lf8_ref, lfp_ref, kc_ref, v_ref):
    a = a_ref[...]
    cos = c_ref[...]
    sin = s_ref[...]
    qn = _rms(a[:, :MLA_Q_RANK], gq_ref[...]).astype(BF16)
    q3 = jnp.dot(qn, wuq_ref[...], preferred_element_type=F32)
    hw = HEADS * HEAD_DIM
    for h in range(HEADS):
        lo, hi = h * HEAD_DIM, (h + 1) * HEAD_DIM
        q_ref[h, :, 0:HEAD_DIM] = q3[:, lo:hi].astype(BF16)
        rot = q3[:, hw + lo:hw + hi] * cos + q3[:, 2 * hw + lo:2 * hw + hi] * sin
        q_ref[h, :, HEAD_DIM:QK_WIDTH] = rot.astype(BF16)

    ckvn = _rms(a[:, MLA_Q_RANK:], gkv_ref[...])
    ckv_ref[...] = ckvn
    sm = sm_ref[...]
    kr128 = sm[:, 0:LANES] * cos + sm[:, LANES:2 * LANES] * sin
    kr_ref[...] = kr128[:, :MLA_ROPE]
    z = sm[:, 2 * LANES:3 * LANES] + bf_ref[...]
    lf = jnp.minimum(z, 0.0) - jnp.log1p(jnp.exp(-jnp.abs(z)))
    lane = lax.broadcasted_iota(jnp.int32, lf.shape, 1)
    lfp_ref[...] = jnp.where(lane < FOX_HEADS, lf, 0.0)
    lf8_ref[...] = lf[:, :FOX_HEADS]
    _kv_up(ckvn, kr128, wukv_ref, kc_ref, v_ref)


def _mla_pre(a3, sm3, cos_t, sin_t, gq, gkv, bf_pad, wuq3, wukv2):
    b, t, wa = a3.shape
    tm = _attn_tile(t)
    ws = sm3.shape[2]
    const = lambda bb, i: (0, 0)
    row3 = lambda bb, i: (bb, i, 0)
    head4 = lambda bb, i: (bb, 0, i, 0)
    vt5 = lambda bb, i: (bb, 0, i, 0, 0)
    return pl.pallas_call(
        _mla_pre_kernel,
        grid=(b, t // tm),
        in_specs=[
            pl.BlockSpec((None, tm, wa), row3),
            pl.BlockSpec((None, tm, ws), row3),
            pl.BlockSpec((tm, LANES), lambda bb, i: (i, 0)),
            pl.BlockSpec((tm, LANES), lambda bb, i: (i, 0)),
            pl.BlockSpec((1, MLA_Q_RANK), const),
            pl.BlockSpec((1, MLA_KV_RANK), const),
            pl.BlockSpec((1, LANES), const),
            pl.BlockSpec(wuq3.shape, const),
            pl.BlockSpec(wukv2.shape, const),
        ],
        out_specs=[
            pl.BlockSpec((None, HEADS, tm, QK_WIDTH), head4),
            pl.BlockSpec((None, tm, MLA_KV_RANK), row3),
            pl.BlockSpec((None, tm, MLA_ROPE), row3),
            pl.BlockSpec((None, tm, FOX_HEADS), row3),
            pl.BlockSpec((None, tm, LANES), row3),
            pl.BlockSpec((None, HEADS, tm, QK_WIDTH), head4),
            pl.BlockSpec((None, HEADS, 1, HEAD_DIM, tm), vt5),
        ],
        out_shape=[
            jax.ShapeDtypeStruct((b, HEADS, t, QK_WIDTH), BF16),
            jax.ShapeDtypeStruct((b, t, MLA_KV_RANK), F32),
            jax.ShapeDtypeStruct((b, t, MLA_ROPE), F32),
            jax.ShapeDtypeStruct((b, t, FOX_HEADS), F32),
            jax.ShapeDtypeStruct((b, t, LANES), F32),
            jax.ShapeDtypeStruct((b, HEADS, t, QK_WIDTH), BF16),
            jax.ShapeDtypeStruct((b, HEADS, t // tm, HEAD_DIM, tm), BF16),
        ],
        compiler_params=_params(2, V7X_VMEM_LIMIT),
        name="mla_pre",
    )(a3, sm3, cos_t, sin_t, gq, gkv, bf_pad, wuq3, wukv2)


def _kv_up_kernel(ckv_ref, kr_ref, wukv_ref, kc_ref, v_ref):
    kr = kr_ref[...]
    kr128 = jnp.concatenate([kr, jnp.zeros((kr.shape[0], LANES - MLA_ROPE), F32)], axis=1)
    _kv_up(ckv_ref[...], kr128, wukv_ref, kc_ref, v_ref)


def _kv_up_call(ckv3, kr3, wukv2):
    b, s, _ = ckv3.shape
    ts = _attn_tile(s)
    row3 = lambda bb, i: (bb, i, 0)
    head4 = lambda bb, i: (bb, 0, i, 0)
    return pl.pallas_call(
        _kv_up_kernel,
        grid=(b, s // ts),
        in_specs=[
            pl.BlockSpec((None, ts, MLA_KV_RANK), row3),
            pl.BlockSpec((None, ts, MLA_ROPE), row3),
            pl.BlockSpec(wukv2.shape, lambda bb, i: (0, 0)),
        ],
        out_specs=[
            pl.BlockSpec((None, HEADS, ts, QK_WIDTH), head4),
            pl.BlockSpec((None, HEADS, 1, HEAD_DIM, ts), lambda bb, i: (bb, 0, i, 0, 0)),
        ],
        out_shape=[
            jax.ShapeDtypeStruct((b, HEADS, s, QK_WIDTH), BF16),
            jax.ShapeDtypeStruct((b, HEADS, s // ts, HEAD_DIM, ts), BF16),
        ],
        compiler_params=_params(2, V7X_VMEM_LIMIT),
        name="kv_up",
    )(ckv3, kr3, wukv2)


def _fox_prep_kernel(*refs, has_q):
    if has_q:
        (q_ref, k_ref, v_ref, lf_ref, init_ref,
         qo_ref, ko_ref, vo_ref, last_ref, carry_ref) = refs
    else:
        (k_ref, v_ref, lf_ref, init_ref, ko_ref, vo_ref, last_ref, carry_ref) = refs
    i = pl.program_id(1)

    @pl.when(i == 0)
    def _():
        carry_ref[...] = init_ref[...]

    lf = lf_ref[...]
    ts = lf.shape[0]
    row = lax.broadcasted_iota(jnp.int32, (ts, ts), 0)
    col = lax.broadcasted_iota(jnp.int32, (ts, ts), 1)
    tri = jnp.where(col <= row, 1.0, 0.0).astype(BF16)
    cum = carry_ref[...]
    for part in _split3(lf):
        cum = cum + jnp.dot(tri, part.astype(BF16), preferred_element_type=F32)
    carry_ref[...] = cum[ts - 1:ts, :]
    last_ref[...] = cum[ts - 1:ts, :]

    c1, c2, c3 = _split3(cum * LOG2E)
    lane = lax.broadcasted_iota(jnp.int32, (ts, LANES), 1)
    k_hm = jnp.swapaxes(k_ref[...], 0, 1)
    v_hm = jnp.swapaxes(v_ref[...], 0, 1)
    for h in range(HEADS):
        lo, hi = h * HEAD_DIM, (h + 1) * HEAD_DIM
        h1, h2, h3 = c1[:, h:h + 1], c2[:, h:h + 1], c3[:, h:h + 1]
        ek = jnp.where(lane < 3, 1.0,
                       jnp.where(lane == 3, -h1,
                                 jnp.where(lane == 4, -h2,
                                           jnp.where(lane == 5, -h3, 0.0))))
        ko_ref[h, :, 0:HEAD_DIM] = k_hm[h].astype(BF16)
        ko_ref[h, :, HEAD_DIM:QK_WIDTH] = ek.astype(BF16)
        vo_ref[h, 0] = _transpose_rows(v_hm[h]).astype(BF16)
        if has_q:
            eq = jnp.where(lane == 0, h1,
                           jnp.where(lane == 1, h2,
                                     jnp.where(lane == 2, h3,
                                               jnp.where(lane < 6, 1.0, 0.0))))
            qo_ref[h, :, 0:HEAD_DIM] = q_ref[:, lo:hi]
            qo_ref[h, :, HEAD_DIM:QK_WIDTH] = eq.astype(BF16)


def _fox_prep(q_src, k_arr, v_arr, lf_pad, init):
    b, s = k_arr.shape[:2]
    ts = _attn_tile(s)
    has_q = q_src is not None
    head4 = lambda bb, i: (bb, 0, i, 0)
    in_specs, args = [], []
    if has_q:
        q_arr, q_blk = q_src
        in_specs.append(pl.BlockSpec((None, ts, FOX_WIDTH), lambda bb, i: (bb, i, q_blk)))
        args.append(q_arr)
    in_specs += [
        pl.BlockSpec((None, ts, FOX_HEADS, HEAD_DIM), lambda bb, i: (bb, i, 0, 0)),
        pl.BlockSpec((None, ts, FOX_HEADS, HEAD_DIM), lambda bb, i: (bb, i, 0, 0)),
        pl.BlockSpec((None, ts, LANES), lambda bb, i: (bb, i, 0)),
        pl.BlockSpec((None, 1, LANES), lambda bb, i: (bb, 0, 0)),
    ]
    args += [k_arr, v_arr, lf_pad, init]
    out_specs, out_shape = [], []
    if has_q:
        out_specs.append(pl.BlockSpec((None, HEADS, ts, QK_WIDTH), head4))
        out_shape.append(jax.ShapeDtypeStruct((b, HEADS, s, QK_WIDTH), BF16))
    out_specs += [
        pl.BlockSpec((None, HEADS, ts, QK_WIDTH), head4),
        pl.BlockSpec((None, HEADS, 1, HEAD_DIM, ts), lambda bb, i: (bb, 0, i, 0, 0)),
        pl.BlockSpec((None, 1, LANES), lambda bb, i: (bb, 0, 0)),
    ]
    out_shape += [
        jax.ShapeDtypeStruct((b, HEADS, s, QK_WIDTH), BF16),
        jax.ShapeDtypeStruct((b, HEADS, s // ts, HEAD_DIM, ts), BF16),
        jax.ShapeDtypeStruct((b, 1, LANES), F32),
    ]
    return pl.pallas_call(
        functools.partial(_fox_prep_kernel, has_q=has_q),
        grid=(b, s // ts),
        in_specs=in_specs,
        out_specs=out_specs,
        out_shape=out_shape,
        scratch_shapes=[pltpu.VMEM((1, LANES), F32)],
        compiler_params=_params(2, V7X_VMEM_LIMIT),
        name="fox_prep_q" if has_q else "fox_prep_past",
    )(*args)


def _attn_kernel(*refs, tq, tkp, n_past, mask_shift, hp):
    n_in = 5 if n_past else 3
    q_ref, kn_ref, vn_ref = refs[:3]
    kp_ref, vp_ref = refs[3:5] if n_past else (None, None)
    o_ref = refs[n_in]
    acc_refs = refs[n_in + 1:]
    i = pl.program_id(2)
    for acc_ref in acc_refs:
        acc_ref[...] = jnp.zeros(acc_ref.shape, F32)

    def block(states, k_ref, off, tk, vt_ref, j, mask):
        scores = [lax.dot_general(k_ref[hh, pl.ds(off, tk), :], q_ref[hh],
                                  (((1,), (1,)), ((), ())), preferred_element_type=F32)
                  for hh in range(hp)]
        out = []
        for hh in range(hp):
            m_prev, l_prev = states[hh]
            s_t = scores[hh]
            if mask is not None:
                s_t = jnp.where(mask, s_t, NEG_INF)
            m_new = jnp.maximum(m_prev, jnp.max(s_t, axis=0, keepdims=True))
            alpha = jnp.exp2(m_prev - m_new)
            p_t = jnp.exp2(s_t - m_new)
            l_new = alpha * l_prev + jnp.sum(p_t, axis=0, keepdims=True)
            acc_refs[hh][...] = alpha * acc_refs[hh][...] + jnp.dot(
                vt_ref[hh, j], p_t.astype(BF16), preferred_element_type=F32)
            out.append((m_new, l_new))
        return tuple(out)

    states = tuple((jnp.full((1, tq), NEG_INF, F32), jnp.zeros((1, tq), F32))
                   for _ in range(hp))

    if n_past:
        def past_body(j, st):
            return block(st, kp_ref, pl.multiple_of(j * tkp, tkp), tkp, vp_ref, j, None)
        states = lax.fori_loop(0, n_past, past_body, states)

    def new_body(j, st):
        return block(st, kn_ref, pl.multiple_of(j * tq, tq), tq, vn_ref, j, None)
    states = lax.fori_loop(0, i, new_body, states)

    key = lax.broadcasted_iota(jnp.int32, (tq, tq), 0)
    qry = lax.broadcasted_iota(jnp.int32, (tq, tq), 1)
    mask = (key >> mask_shift) <= (qry >> mask_shift)
    states = block(states, kn_ref, pl.multiple_of(i * tq, tq), tq, vn_ref, i, mask)
    for hh in range(hp):
        o_t = acc_refs[hh][...] / states[hh][1]
        o_ref[:, hh * HEAD_DIM:(hh + 1) * HEAD_DIM] = _transpose_cols(o_t).astype(o_ref.dtype)


def _attention(q4, kn4, vn5, past, mask_shift):
    b, h, t, _ = q4.shape
    tq = _attn_tile(t)
    hp = ATTN_HEADS_PER_STEP
    assert h % hp == 0 and vn5.shape[4] == tq
    in_specs = [
        pl.BlockSpec((None, hp, tq, QK_WIDTH), lambda bb, g, i: (bb, g, i, 0)),
        pl.BlockSpec((None, hp, t, QK_WIDTH), lambda bb, g, i: (bb, g, 0, 0)),
        pl.BlockSpec((None, hp, t // tq, HEAD_DIM, tq), lambda bb, g, i: (bb, g, 0, 0, 0)),
    ]
    args = [q4, kn4, vn5]
    n_past, tkp = 0, 0
    if past is not None:
        kp4, vp5 = past
        p_len = kp4.shape[2]
        n_past, tkp = vp5.shape[2], vp5.shape[4]
        in_specs += [
            pl.BlockSpec((None, hp, p_len, QK_WIDTH), lambda bb, g, i: (bb, g, 0, 0)),
            pl.BlockSpec((None, hp, n_past, HEAD_DIM, tkp), lambda bb, g, i: (bb, g, 0, 0, 0)),
        ]
        args += [kp4, vp5]
    kern = functools.partial(_attn_kernel, tq=tq, tkp=tkp, n_past=n_past,
                             mask_shift=mask_shift, hp=hp)
    return pl.pallas_call(
        kern,
        grid=(b, h // hp, t // tq),
        in_specs=in_specs,
        out_specs=pl.BlockSpec((None, tq, hp * HEAD_DIM), lambda bb, g, i: (bb, i, g)),
        out_shape=jax.ShapeDtypeStruct((b, t, h * HEAD_DIM), BF16),
        scratch_shapes=[pltpu.VMEM((HEAD_DIM, tq), F32) for _ in range(hp)],
        compiler_params=_params(3, V7X_VMEM_LIMIT),
        name="attn_chunk" if mask_shift else "attn_frame",
    )(*args)


def _out_merge_kernel(oa_ref, ob_ref, ga_ref, gb_ref, x_ref, woa_ref, wob_ref, wo_ref,
                      h_ref, mg_ref, *, tc):
    d = x_ref.shape[1]
    oa = oa_ref[...]
    ob = ob_ref[...]
    for c in range(d // tc):
        sl = slice(c * tc, (c + 1) * tc)
        ta = jnp.dot(oa, woa_ref[:, sl], preferred_element_type=F32)
        tb = jnp.dot(ob, wob_ref[:, sl], preferred_element_type=F32)
        ga = jax.nn.sigmoid(ga_ref[:, sl].astype(F32))
        gb = jax.nn.sigmoid(gb_ref[:, sl].astype(F32))
        mg_ref[:, sl] = (ga * ta + gb * tb).astype(BF16)
    mg = mg_ref[...]
    for c in range(d // tc):
        sl = slice(c * tc, (c + 1) * tc)
        h_ref[:, sl] = x_ref[:, sl] + jnp.dot(mg, wo_ref[:, sl], preferred_element_type=F32)


def _out_merge(oa, ob, g_arr, x2d, woa, wob, wo):
    n, d = x2d.shape
    tm = _tile(n, 256)
    tc = _tile(d, 512)
    const = lambda i: (0, 0)
    one = pl.Buffered(1)
    return pl.pallas_call(
        functools.partial(_out_merge_kernel, tc=tc),
        grid=(n // tm,),
        in_specs=[
            pl.BlockSpec((tm, oa.shape[1]), lambda i: (i, 0)),
            pl.BlockSpec((tm, ob.shape[1]), lambda i: (i, 0)),
            pl.BlockSpec((tm, d), lambda i: (i, 0)),
            pl.BlockSpec((tm, d), lambda i: (i, 1)),
            pl.BlockSpec((tm, d), lambda i: (i, 0)),
            pl.BlockSpec(woa.shape, const, pipeline_mode=one),
            pl.BlockSpec(wob.shape, const, pipeline_mode=one),
            pl.BlockSpec(wo.shape, const, pipeline_mode=one),
        ],
        out_specs=pl.BlockSpec((tm, d), lambda i: (i, 0)),
        out_shape=jax.ShapeDtypeStruct((n, d), F32),
        scratch_shapes=[pltpu.VMEM((tm, d), BF16)],
        compiler_params=_params(1, V7X_VMEM_LIMIT),
        name="out_merge",
    )(oa, ob, g_arr, g_arr, x2d, woa, wob, wo)


def _ffn_kernel(h_ref, g_ref, wuv_ref, wug_ref, pv_ref, pg_ref, wcv_ref, wcg_ref,
                bcv_ref, bcg_ref, wd_ref, o_ref, ncv_ref, ncg_ref,
                hn_ref, acc_ref, act0_ref, act1_ref, cv_ref, cg_ref,
                *, seg, tiles_per_seq, nf):
    act_refs = (act0_ref, act1_ref)
    i = pl.program_id(0)
    j = pl.program_id(1)
    tm = h_ref.shape[0]
    nseg = tm // seg
    keep = CONV_W - 1
    seq_start = (i % tiles_per_seq) == 0

    def conv(w_ref, prev_ref, carry_ref, wc_ref, bc_ref, nc_ref):
        u = jnp.dot(hn_ref[...], w_ref[...], preferred_element_type=F32)
        wc = wc_ref[...]
        bc = bc_ref[...]
        outs = []
        for s in range(nseg):
            us = u[s * seg:(s + 1) * seg, :]
            tail = us[seg - keep:seg, :]
            if nseg == 1:
                halo = jnp.where(seq_start, prev_ref[0], carry_ref[j])
                carry_ref[j] = tail
            else:
                halo = prev_ref[s]
            nc_ref[s] = tail
            r1 = pltpu.roll(us, 1, axis=0)
            r2 = pltpu.roll(us, 2, axis=0)
            row = lax.broadcasted_iota(jnp.int32, (SUBLANES, us.shape[1]), 0)
            top1 = jnp.where(row == 0, halo[1:2, :], r1[:SUBLANES, :])
            top2 = jnp.where(row == 0, halo[0:1, :],
                             jnp.where(row == 1, halo[1:2, :], r2[:SUBLANES, :]))
            u1 = jnp.concatenate([top1, r1[SUBLANES:, :]], axis=0)
            u2 = jnp.concatenate([top2, r2[SUBLANES:, :]], axis=0)
            outs.append(bc + u2 * wc[0:1, :] + u1 * wc[1:2, :] + us * wc[2:3, :])
        return outs

    def up(slot):
        half_vals = conv(wuv_ref, pv_ref, cv_ref, wcv_ref, bcv_ref, ncv_ref)
        gates = conv(wug_ref, pg_ref, cg_ref, wcg_ref, bcg_ref, ncg_ref)
        for s in range(nseg):
            gate = gates[s]
            z = gate * (GELU_C * GELU_A * (gate * gate) + GELU_C)
            act = (gate * half_vals[s]) * (1.0 + jnp.tanh(z))
            act_refs[slot][s * seg:(s + 1) * seg, :] = act.astype(BF16)

    def down(slot):
        acc_ref[...] += jnp.dot(act_refs[slot][...], wd_ref[...], preferred_element_type=F32)

    @pl.when(j == 0)
    def _():
        hn_ref[...] = _rms(h_ref[...], g_ref[...]).astype(BF16)
        acc_ref[...] = jnp.zeros(acc_ref.shape, F32)
        up(0)

    for parity in range(2):
        @pl.when(jnp.logical_and(jnp.logical_and(j >= 1, j < nf), j % 2 == parity))
        def _():
            up(parity)
            down(1 - parity)

    @pl.when(j == nf)
    def _():
        down((nf - 1) % 2)
        o_ref[...] = h_ref[...] + acc_ref[...]


def _ffn(h2d, prev, g, w_up, w_conv, b_conv, w_down, t):
    n, d = h2d.shape
    f = w_down.shape[0]
    b = n // t
    keep = CONV_W - 1
    assert CONV_W == 3
    tm = _tile(n, 512)
    if t >= tm:
        tm = _tile(t, 512)
        seg, nb_t, tiles_per_seq = tm, 1, t // tm
        bidx = lambda i: i // tiles_per_seq
    else:
        assert tm % t == 0
        seg, nb_t, tiles_per_seq = t, tm // t, 1
        bidx = lambda i: i
    tc = _ffn_tc(f)
    nf = f // tc
    assert w_up.shape == (d, 2 * f)
    up_j = lambda j: jnp.minimum(j, nf - 1)
    down_j = lambda j: jnp.maximum(j - 1, 0)
    h2, ncv, ncg = pl.pallas_call(
        functools.partial(_ffn_kernel, seg=seg, tiles_per_seq=tiles_per_seq, nf=nf),
        grid=(n // tm, nf + 1),
        in_specs=[
            pl.BlockSpec((tm, d), lambda i, j: (i, 0)),
            pl.BlockSpec((1, d), lambda i, j: (0, 0)),
            pl.BlockSpec((d, tc), lambda i, j: (0, up_j(j))),
            pl.BlockSpec((d, tc), lambda i, j: (0, up_j(j) + nf)),
            pl.BlockSpec((nb_t, keep, tc), lambda i, j: (bidx(i), 0, up_j(j))),
            pl.BlockSpec((nb_t, keep, tc), lambda i, j: (bidx(i), 0, up_j(j) + nf)),
            pl.BlockSpec((CONV_W, tc), lambda i, j: (0, up_j(j))),
            pl.BlockSpec((CONV_W, tc), lambda i, j: (0, up_j(j) + nf)),
            pl.BlockSpec((1, tc), lambda i, j: (0, up_j(j))),
            pl.BlockSpec((1, tc), lambda i, j: (0, up_j(j) + nf)),
            pl.BlockSpec((tc, d), lambda i, j: (down_j(j), 0)),
        ],
        out_specs=[
            pl.BlockSpec((tm, d), lambda i, j: (i, 0)),
            pl.BlockSpec((nb_t, keep, tc), lambda i, j: (i, 0, up_j(j))),
            pl.BlockSpec((nb_t, keep, tc), lambda i, j: (i, 0, up_j(j))),
        ],
        out_shape=[
            jax.ShapeDtypeStruct((n, d), F32),
            jax.ShapeDtypeStruct((b * tiles_per_seq, keep, f), F32),
            jax.ShapeDtypeStruct((b * tiles_per_seq, keep, f), F32),
        ],
        scratch_shapes=[
            pltpu.VMEM((tm, d), BF16),
            pltpu.VMEM((tm, d), F32),
            pltpu.VMEM((tm, tc), BF16),
            pltpu.VMEM((tm, tc), BF16),
            pltpu.VMEM((nf, keep, tc), F32),
            pltpu.VMEM((nf, keep, tc), F32),
        ],
        compiler_params=_params(2, V7X_VMEM_LIMIT),
        name="ffn",
    )(h2d, g, w_up, w_up, prev, prev, w_conv, w_conv, b_conv, b_conv, w_down)
    last = lambda a: a.reshape(b, tiles_per_seq, keep, f)[:, tiles_per_seq - 1]
    return h2, jnp.concatenate([last(ncv), last(ncg)], axis=-1)


def _ple_final_kernel(h_ref, pe_ref, gp_ref, gf_ref, wpg_ref, wple_ref, y_ref, h3_ref, *, tc):
    d = h_ref.shape[1]
    hn = _rms(h_ref[...], gp_ref[...]).astype(BF16)
    pe = pe_ref[...].astype(BF16)
    for c in range(d // tc):
        sl = slice(c * tc, (c + 1) * tc)
        gate = jax.nn.sigmoid(jnp.dot(hn, wpg_ref[:, sl], preferred_element_type=F32))
        val = jnp.dot(pe, wple_ref[:, sl], preferred_element_type=F32)
        h3_ref[:, sl] = h_ref[:, sl] + gate * val
    y_ref[...] = _rms(h3_ref[...], gf_ref[...])


def _ple_final(h2d, pe2d, g_ple, g_final, wpg, wple):
    n, d = h2d.shape
    tm = _tile(n, 256)
    tc = _tile(d, 512)
    const = lambda i: (0, 0)
    one = pl.Buffered(1)
    return pl.pallas_call(
        functools.partial(_ple_final_kernel, tc=tc),
        grid=(n // tm,),
        in_specs=[
            pl.BlockSpec((tm, d), lambda i: (i, 0)),
            pl.BlockSpec((tm, pe2d.shape[1]), lambda i: (i, 0)),
            pl.BlockSpec((1, d), const),
            pl.BlockSpec((1, d), const),
            pl.BlockSpec(wpg.shape, const, pipeline_mode=one),
            pl.BlockSpec(wple.shape, const, pipeline_mode=one),
        ],
        out_specs=pl.BlockSpec((tm, d), lambda i: (i, 0)),
        out_shape=jax.ShapeDtypeStruct((n, d), F32),
        scratch_shapes=[pltpu.VMEM((tm, d), F32)],
        compiler_params=_params(1, V7X_VMEM_LIMIT),
        name="ple_final",
    )(h2d, pe2d, g_ple, g_final, wpg, wple)


def _cast_kernel(x_ref, o_ref):
    o_ref[...] = x_ref[...].astype(o_ref.dtype)


def _cast_rows(r, c):
    return _tile(r, max(SUBLANES, ((8 << 20) // (4 * c)) // SUBLANES * SUBLANES))


def _to_bf16(w):
    r, c = w.shape
    tr = _cast_rows(r, c)
    return pl.pallas_call(
        _cast_kernel,
        grid=(r // tr,),
        in_specs=[pl.BlockSpec((tr, c), lambda i: (i, 0))],
        out_specs=pl.BlockSpec((tr, c), lambda i: (i, 0)),
        out_shape=jax.ShapeDtypeStruct((r, c), BF16),
        compiler_params=_params(1, V7X_VMEM_LIMIT),
        name="cast_bf16",
    )(w)


def _regroup_kernel(x_ref, o_ref, *, moves):
    x = x_ref[...]
    for dst, src, width, scale in moves:
        v = x[:, src:src + width]
        if scale is not None:
            v = v * scale
        o_ref[:, dst:dst + width] = v.astype(o_ref.dtype)


def _regroup_w_in(w_in, moves, width):
    r, c = w_in.shape
    tr = _cast_rows(r, c)
    return pl.pallas_call(
        functools.partial(_regroup_kernel, moves=tuple(moves)),
        grid=(r // tr,),
        in_specs=[pl.BlockSpec((tr, c), lambda i: (i, 0))],
        out_specs=pl.BlockSpec((tr, width), lambda i: (i, 0)),
        out_shape=jax.ShapeDtypeStruct((r, width), BF16),
        compiler_params=_params(1, V7X_VMEM_LIMIT),
        name="regroup_w_in",
    )(w_in)


def _prep_weights(w, d_model):
    half = MLA_ROPE // 2
    o_cq = 0
    o_ckv = o_cq + MLA_Q_RANK
    o_kr = o_ckv + MLA_KV_RANK
    o_fq = o_kr + MLA_ROPE
    o_fk = o_fq + FOX_WIDTH
    o_fv = o_fk + FOX_WIDTH
    o_fl = o_fv + FOX_WIDTH
    o_ga = o_fl + FOX_HEADS
    o_gb = o_ga + d_model
    w_in = w["w_in"]
    d = w_in.shape[0]
    col = lambda o, n: w_in[:, o:o + n]
    moves, dst = [], 0
    for src, width, scale in [(o_cq, MLA_Q_RANK, None), (o_ckv, MLA_KV_RANK, None),
                              (o_fk, FOX_WIDTH, None), (o_fv, FOX_WIDTH, None),
                              (o_ga, d_model, None), (o_gb, d_model, None),
                              (o_fq, FOX_WIDTH, FOX_SCALE * LOG2E)]:
        moves.append((dst, src, width, scale))
        dst += width
    w_all = _regroup_w_in(w_in, moves, dst)
    zeros = lambda n: jnp.zeros((d, n), F32)
    w_s = jnp.concatenate([
        col(o_kr, MLA_ROPE), zeros(LANES - MLA_ROPE),
        col(o_kr + half, half), col(o_kr, half), zeros(LANES - MLA_ROPE),
        col(o_fl, FOX_HEADS), zeros(LANES - FOX_HEADS),
    ], axis=1).astype(BF16)

    qk = MLA_NOPE + MLA_ROPE
    wq = w["w_uq"].reshape(MLA_Q_RANK, HEADS, qk) * (MLA_SCALE * LOG2E)
    pad = jnp.zeros((MLA_Q_RANK, HEADS, LANES - MLA_ROPE), F32)
    nope = wq[:, :, :MLA_NOPE]
    x1 = wq[:, :, MLA_NOPE:MLA_NOPE + half]
    x2 = wq[:, :, MLA_NOPE + half:]
    rot = jnp.concatenate([x1, x2, pad], axis=2)
    rot_sw = jnp.concatenate([x2, x1, pad], axis=2)
    wuq3 = jnp.concatenate([nope.reshape(MLA_Q_RANK, -1), rot.reshape(MLA_Q_RANK, -1),
                            rot_sw.reshape(MLA_Q_RANK, -1)], axis=1).astype(BF16)
    wkv = w["w_ukv"].reshape(MLA_KV_RANK, HEADS, MLA_NOPE + MLA_V)
    wukv2 = jnp.concatenate([wkv[:, :, :MLA_NOPE].reshape(MLA_KV_RANK, -1),
                             wkv[:, :, MLA_NOPE:].reshape(MLA_KV_RANK, -1)], axis=1).astype(BF16)
    bf_pad = jnp.zeros((1, LANES), F32).at[0, :FOX_HEADS].set(w["b_f"])
    f = w["w_down"].shape[0]
    halve = jnp.concatenate([jnp.full((f,), 0.5, F32), jnp.ones((f,), F32)])
    return dict(
        w_all=w_all, w_s=w_s, wuq3=wuq3, wukv2=wukv2, bf_pad=bf_pad,
        g_mix=w["g_mix"][None, :], g_q=w["g_q"][None, :], g_kv=w["g_kv"][None, :],
        w_oa=_to_bf16(w["w_oa"]), w_ob=_to_bf16(w["w_ob"]), w_o=_to_bf16(w["w_o"]),
        g_ffn=w["g_ffn"][None, :], w_up=_to_bf16(w["w_up"]),
        w_conv=w["w_conv"] * halve, b_conv=(w["b_conv"] * halve)[None, :],
        w_down=_to_bf16(w["w_down"]),
        g_ple=w["g_ple"][None, :], w_pg=_to_bf16(w["w_pg"]), w_ple=_to_bf16(w["w_ple"]),
    )


def _rope_tables(pos):
    half = MLA_ROPE // 2
    inv = ROPE_THETA ** (-jnp.arange(half, dtype=F32) / half)
    ang = pos.astype(F32)[:, None] * inv[None, :]
    cos, sin = jnp.cos(ang), jnp.sin(ang)
    zero = jnp.zeros((pos.shape[0], LANES - MLA_ROPE), F32)
    return (jnp.concatenate([cos, cos, zero], axis=1),
            jnp.concatenate([-sin, sin, zero], axis=1))


def _layer(x, pe, past, pw, g_final):
    b, t, d = x.shape
    n = b * t
    p_len = 0 if past is None else past[0].shape[1]
    x2d = x.reshape(n, d)

    sm2d, a2d, fk, fv, g2d = _in_proj(x2d, pw["g_mix"], pw["w_all"], pw["w_s"], d)
    fk = fk.reshape(b, t, FOX_HEADS, FOX_HEAD_DIM)
    fv = fv.reshape(b, t, FOX_HEADS, FOX_HEAD_DIM)
    cos_t, sin_t = _rope_tables(p_len + jnp.arange(t))
    (q_mla, ckv_n, kr_n, logf, logf_pad, kc_new, v_new) = _mla_pre(
        a2d.reshape(b, t, -1), sm2d.reshape(b, t, -1), cos_t, sin_t,
        pw["g_q"], pw["g_kv"], pw["bf_pad"], pw["wuq3"], pw["wukv2"])

    g3 = g2d.reshape(b, t, -1)
    fq_blk = (2 * d) // FOX_WIDTH
    assert (2 * d) % FOX_WIDTH == 0
    if past is None:
        mla_past = None
        fox_past = None
        init = jnp.zeros((b, 1, LANES), F32)
    else:
        c_ckv, c_kr, c_fk, c_fv, c_logf, _ = past
        mla_past = _kv_up_call(c_ckv, c_kr, pw["wukv2"])
        lf_past = jnp.pad(c_logf, ((0, 0), (0, 0), (0, LANES - FOX_HEADS)))
        kp, vp, init = _fox_prep(None, c_fk, c_fv, lf_past, jnp.zeros((b, 1, LANES), F32))
        fox_past = (kp, vp)
    q_fox, k_fox, v_fox, _ = _fox_prep((g3, fq_blk), fk, fv, logf_pad, init)

    o_a = _attention(q_mla, kc_new, v_new, mla_past, int(math.log2(CHUNK)))
    o_b = _attention(q_fox, k_fox, v_fox, fox_past, 0)

    h2d = _out_merge(o_a.reshape(n, -1), o_b.reshape(n, -1), g2d, x2d,
                     pw["w_oa"], pw["w_ob"], pw["w_o"])

    f2 = 2 * pw["w_down"].shape[0]
    prev = (jnp.zeros((b, CONV_W - 1, f2), F32) if past is None else past[5])
    h2d, new_conv = _ffn(h2d, prev, pw["g_ffn"], pw["w_up"], pw["w_conv"], pw["b_conv"],
                         pw["w_down"], t)

    y2d = _ple_final(h2d, pe.reshape(n, -1), pw["g_ple"], g_final[None, :], pw["w_pg"], pw["w_ple"])

    return y2d.reshape(b, t, d), (ckv_n, kr_n, fk, fv, logf, new_conv)


def kernel(x_prompt, x_sample, cache_mla_ckv, cache_mla_krope, cache_fox_k, cache_fox_v,
           cache_fox_logf, state_ffn_conv, p_prompt, p_sample, g_mix, w_in, b_f, g_q, w_uq,
           g_kv, w_ukv, w_oa, w_ob, w_o, g_ffn, w_up, w_conv, b_conv, w_down, g_ple, w_pg,
           w_ple, g_final):
    depth = w_in.shape[0]
    assert depth == 1, "the final norm is fused into the layer's last kernel"
    d_model = x_prompt.shape[-1]
    layer0 = lambda a: a.reshape(a.shape[1:])
    stacked = lambda a: a.reshape((1,) + a.shape)
    w = {"g_mix": g_mix, "w_in": w_in, "b_f": b_f, "g_q": g_q, "w_uq": w_uq,
         "g_kv": g_kv, "w_ukv": w_ukv, "w_oa": w_oa, "w_ob": w_ob, "w_o": w_o,
         "g_ffn": g_ffn, "w_up": w_up, "w_conv": w_conv, "b_conv": b_conv,
         "w_down": w_down, "g_ple": g_ple, "w_pg": w_pg, "w_ple": w_ple}
    pw = _prep_weights({k: layer0(v) for k, v in w.items()}, d_model)

    y_p, st_p = _layer(x_prompt, layer0(p_prompt), None, pw, g_final)
    past = tuple(layer0(a) for a in (cache_mla_ckv, cache_mla_krope, cache_fox_k, cache_fox_v,
                                     cache_fox_logf, state_ffn_conv))
    y_s, st_s = _layer(x_sample, layer0(p_sample), past, pw, g_final)

    outs = [y_p, y_s]
    for j in range(6):
        outs.append(stacked(st_p[j]))
        outs.append(stacked(st_s[j]))
    return tuple(outs)
```
